```python
import jax, jax.numpy as jnp
from jax import lax
import numpy as np

D_MODEL = 2048
BATCH = 4
SEQ = 8192
DEPTH = 1

D_MIX = D_MODEL
RWKV_WIDTH = D_MIX // 2
LRU_WIDTH = D_MIX - RWKV_WIDTH
HEAD_DIM = 64
RWKV_HEADS = RWKV_WIDTH // HEAD_DIM
LRU_BLOCKS = 16
LRU_BLOCK_DIM = LRU_WIDTH // LRU_BLOCKS
DECAY_LORA = max(32, int(round(1.8 * RWKV_WIDTH ** 0.5 / 32)) * 32)
AAA_LORA = max(32, int(round(1.8 * RWKV_WIDTH ** 0.5 / 32)) * 32)
GATE_LORA = max(32, int(round(0.6 * RWKV_WIDTH ** 0.8 / 32)) * 32)
RWKV_IN = 3 * RWKV_WIDTH + DECAY_LORA + AAA_LORA + GATE_LORA
LRU_IN = 2 * LRU_WIDTH
D_IN = RWKV_IN + LRU_IN
D_FF = 11 * D_MODEL // 4
CONV_WIDTH = 4
LRU_C = 8.0
NORM_EPS = 1e-6
GN_EPS = 64e-5

kernel_name = "hybrid_rwkv7_rglru_macaron"


def _split(t, sizes):
    out, start = [], 0
    for s in sizes:
        out.append(t[..., start:start + s])
        start += s
    return out


def rms_norm(x, g):
    xf = x.astype(jnp.float32)
    y = xf * lax.rsqrt(jnp.mean(xf * xf, axis=-1, keepdims=True) + NORM_EPS)
    return (y * g.astype(jnp.float32)).astype(x.dtype)


def swiglu(x, w_gate, w_up, w_down):
    return (jax.nn.silu(x @ w_gate) * (x @ w_up)) @ w_down


def shift_prev(t):
    return jnp.pad(t, ((0, 0), (1, 0), (0, 0)))[:, :-1]


def rwkv7_time_mix(p, mu, w0, w2, a0, a2, g2, k_k, k_a, r_k, ln_w, ln_b):
    B, S, _ = p.shape
    f32 = jnp.float32
    p = p + (shift_prev(p) - p) * mu
    r, xw, k, v, xa, xg = _split(p, (RWKV_WIDTH, DECAY_LORA, RWKV_WIDTH, RWKV_WIDTH, AAA_LORA, GATE_LORA))
    w = -jax.nn.softplus(-(w0 + jnp.tanh(xw) @ w2)) - 0.5
    a = jax.nn.sigmoid(a0 + xa @ a2)
    g = jax.nn.sigmoid(xg) @ g2
    heads = lambda t: t.astype(f32).reshape(B, S, RWKV_HEADS, HEAD_DIM)
    kk = heads(k * k_k)
    kk = kk / jnp.maximum(jnp.sqrt(jnp.sum(kk * kk, axis=-1, keepdims=True)), 1e-12)
    k = heads(k * (1.0 + (a - 1.0) * k_a))
    r, v, a = heads(r), heads(v), heads(a)
    decay = jnp.exp(-jnp.exp(heads(w)))
    tm = lambda t: jnp.swapaxes(t, 0, 1)

    def step(state, inp):
        r_t, d_t, k_t, v_t, kk_t, a_t = inp
        sa = jnp.einsum('bhij,bhj->bhi', state, -kk_t)
        state = (state * d_t[:, :, None, :]
                 + sa[..., None] * (kk_t * a_t)[:, :, None, :]
                 + v_t[..., None] * k_t[:, :, None, :])
        return state, jnp.einsum('bhij,bhj->bhi', state, r_t)

    state0 = jnp.zeros((B, RWKV_HEADS, HEAD_DIM, HEAD_DIM), f32)
    _, y = lax.scan(step, state0, (tm(r), tm(decay), tm(k), tm(v), tm(kk), tm(a)))
    y = tm(y)
    mean = jnp.mean(y, axis=-1, keepdims=True)
    var = jnp.mean(jnp.square(y - mean), axis=-1, keepdims=True)
    y = ((y - mean) * lax.rsqrt(var + GN_EPS)).reshape(B, S, RWKV_WIDTH)
    y = y * ln_w.astype(f32) + ln_b.astype(f32)
    bonus = jnp.sum(r * k * r_k.astype(f32), axis=-1, keepdims=True) * v
    y = y + bonus.reshape(B, S, RWKV_WIDTH)
    return (y * g.astype(f32)).astype(p.dtype)


def rglru_mix(p, conv_w, conv_b, wa, ba, wx, bx, lam, norm_g):
    B, S, _ = p.shape
    f32 = jnp.float32
    xb, gate = _split(p, (LRU_WIDTH, LRU_WIDTH))
    xc = lax.conv_general_dilated(
        xb, conv_w[:, None, :], window_strides=(1,), padding=[(CONV_WIDTH - 1, 0)],
        dimension_numbers=('NWC', 'WIO', 'NWC'), feature_group_count=LRU_WIDTH) + conv_b
    blocks = xc.reshape(B, S, LRU_BLOCKS, LRU_BLOCK_DIM)
    r = jax.nn.sigmoid(jnp.einsum('bshi,hij->bshj', blocks, wa).reshape(B, S, LRU_WIDTH) + ba)
    i = jax.nn.sigmoid(jnp.einsum('bshi,hij->bshj', blocks, wx).reshape(B, S, LRU_WIDTH) + bx)
    log_a = -LRU_C * r.astype(f32) * jax.nn.softplus(-lam.astype(f32))
    a = jnp.exp(log_a)
    u = jnp.sqrt(-jnp.expm1(2.0 * log_a)) * (i * xc).astype(f32)

    def step(h, au):
        a_t, u_t = au
        h = a_t * h + u_t
        return h, h

    _, h = lax.scan(step, jnp.zeros((B, LRU_WIDTH), f32),
                    (jnp.swapaxes(a, 0, 1), jnp.swapaxes(u, 0, 1)))
    h = jnp.swapaxes(h, 0, 1).astype(p.dtype)
    y = h * jax.nn.gelu(gate)
    return rms_norm(y, norm_g)


def setup_inputs(seed: int = 0) -> dict:
    key = jax.random.key(seed)
    ks = iter(jax.random.split(key, 40))
    f32 = jnp.float32
    L = DEPTH
    nrm = lambda shape, scale: jax.random.normal(next(ks), shape, f32) * scale
    gain = lambda shape: 1.0 + nrm(shape, 0.02)
    uni = lambda shape, lo, hi: jax.random.uniform(next(ks), shape, f32, lo, hi)
    x = nrm((BATCH, SEQ, D_MODEL), 1.0)
    d = {"x": x}
    d["ffn1_norm"] = gain((L, D_MODEL))
    d["ffn1_w_gate"] = nrm((L, D_MODEL, D_FF), D_MODEL ** -0.5)
    d["ffn1_w_up"] = nrm((L, D_MODEL, D_FF), D_MODEL ** -0.5)
    d["ffn1_w_down"] = nrm((L, D_FF, D_MODEL), D_FF ** -0.5)
    d["mix_norm"] = gain((L, D_MODEL))
    d["w_in"] = nrm((L, D_MODEL, D_IN), D_MODEL ** -0.5)
    d["rwkv_mu"] = uni((L, RWKV_IN), 0.0, 1.0)
    d["rwkv_w0"] = uni((L, RWKV_WIDTH), -6.0, 1.0)
    d["rwkv_w2"] = nrm((L, DECAY_LORA, RWKV_WIDTH), DECAY_LORA ** -0.5)
    d["rwkv_a0"] = nrm((L, RWKV_WIDTH), 0.1)
    d["rwkv_a2"] = nrm((L, AAA_LORA, RWKV_WIDTH), AAA_LORA ** -0.5)
    d["rwkv_g2"] = nrm((L, GATE_LORA, RWKV_WIDTH), GATE_LORA ** -0.5)
    d["rwkv_k_k"] = 0.85 + nrm((L, RWKV_WIDTH), 0.02)
    d["rwkv_k_a"] = gain((L, RWKV_WIDTH))
    d["rwkv_r_k"] = nrm((L, RWKV_HEADS, HEAD_DIM), 0.1)
    d["rwkv_ln_w"] = gain((L, RWKV_WIDTH))
    d["rwkv_ln_b"] = nrm((L, RWKV_WIDTH), 0.01)
    d["lru_conv_w"] = nrm((L, CONV_WIDTH, LRU_WIDTH), 0.5)
    d["lru_conv_b"] = nrm((L, LRU_WIDTH), 0.01)
    d["lru_wa"] = nrm((L, LRU_BLOCKS, LRU_BLOCK_DIM, LRU_BLOCK_DIM), LRU_BLOCK_DIM ** -0.5)
    d["lru_ba"] = nrm((L, LRU_WIDTH), 0.01)
    d["lru_wx"] = nrm((L, LRU_BLOCKS, LRU_BLOCK_DIM, LRU_BLOCK_DIM), LRU_BLOCK_DIM ** -0.5)
    d["lru_bx"] = nrm((L, LRU_WIDTH), 0.01)
    a_pow = uni((L, LRU_WIDTH), 0.9, 0.999)
    a_base = a_pow ** (1.0 / LRU_C)
    d["lru_lam"] = jnp.log(a_base) - jnp.log1p(-a_base)
    d["lru_norm"] = gain((L, LRU_WIDTH))
    d["w_out"] = nrm((L, D_MIX, D_MODEL), D_MIX ** -0.5)
    d["ffn2_norm"] = gain((L, D_MODEL))
    d["ffn2_w_gate"] = nrm((L, D_MODEL, D_FF), D_MODEL ** -0.5)
    d["ffn2_w_up"] = nrm((L, D_MODEL, D_FF), D_MODEL ** -0.5)
    d["ffn2_w_down"] = nrm((L, D_FF, D_MODEL), D_FF ** -0.5)
    d["final_norm"] = gain((D_MODEL,))
    return d


def reference(x, ffn1_norm, ffn1_w_gate, ffn1_w_up, ffn1_w_down,
              mix_norm, w_in, rwkv_mu, rwkv_w0, rwkv_w2, rwkv_a0, rwkv_a2, rwkv_g2,
              rwkv_k_k, rwkv_k_a, rwkv_r_k, rwkv_ln_w, rwkv_ln_b,
              lru_conv_w, lru_conv_b, lru_wa, lru_ba, lru_wx, lru_bx, lru_lam, lru_norm,
              w_out, ffn2_norm, ffn2_w_gate, ffn2_w_up, ffn2_w_down, final_norm):
    h = x
    for l in range(DEPTH):
        h = h + 0.5 * swiglu(rms_norm(h, ffn1_norm[l]), ffn1_w_gate[l], ffn1_w_up[l], ffn1_w_down[l])
        p = rms_norm(h, mix_norm[l]) @ w_in[l]
        y_rwkv = rwkv7_time_mix(p[..., :RWKV_IN], rwkv_mu[l], rwkv_w0[l], rwkv_w2[l],
                                rwkv_a0[l], rwkv_a2[l], rwkv_g2[l], rwkv_k_k[l], rwkv_k_a[l],
                                rwkv_r_k[l], rwkv_ln_w[l], rwkv_ln_b[l])
        y_lru = rglru_mix(p[..., RWKV_IN:], lru_conv_w[l], lru_conv_b[l], lru_wa[l], lru_ba[l],
                          lru_wx[l], lru_bx[l], lru_lam[l], lru_norm[l])
        h = h + jnp.concatenate([y_rwkv, y_lru], axis=-1) @ w_out[l]
        h = h + 0.5 * swiglu(rms_norm(h, ffn2_norm[l]), ffn2_w_gate[l], ffn2_w_up[l], ffn2_w_down[l])
    return rms_norm(h, final_norm)
```

```python
import functools

import jax
import jax.numpy as jnp
from jax import lax
from jax.experimental import pallas as pl
from jax.experimental.pallas import tpu as pltpu

F32 = jnp.float32
BF16 = jnp.bfloat16

HEAD_DIM = 64
CONV_WIDTH = 4
LRU_C = 8.0
NORM_EPS = 1e-6
GN_EPS = 64e-5

LANES = 128
CHUNK = 64
PAIR = LANES // HEAD_DIM

VMEM_LIMIT = 56 * 1024 * 1024


def _dot(a, b):
    return jnp.dot(a, b, preferred_element_type=F32)


def _dot_nt(a, b):
    return lax.dot_general(a, b, (((1,), (1,)), ((), ())), preferred_element_type=F32)


def _dot_tn(a, b):
    return lax.dot_general(a, b, (((0,), (0,)), ((), ())), preferred_element_type=F32)


def _sigmoid(x):
    return 0.5 * (jnp.tanh(0.5 * x) + 1.0)


def _softplus(x):
    return jnp.maximum(x, 0.0) + jnp.log(1.0 + jnp.exp(-jnp.abs(x)))


def _rms_norm(x, g):
    ms = jnp.mean(x * x, axis=-1, keepdims=True)
    return x * lax.rsqrt(ms + NORM_EPS) * g


def _ffn_kernel(h_ref, g_ref, wg_ref, wu_ref, wd_ref, fg_ref, o_ref, xn_ref, acc_ref, *, final_norm):
    j = pl.program_id(1)

    @pl.when(j == 0)
    def _():
        xn_ref[...] = _rms_norm(h_ref[...], g_ref[...]).astype(BF16)
        acc_ref[...] = jnp.zeros_like(acc_ref)

    xn = xn_ref[...]
    gate = _dot(xn, wg_ref[...])
    up = _dot(xn, wu_ref[...])
    act = (gate * _sigmoid(gate) * up).astype(BF16)
    acc_ref[...] += _dot(act, wd_ref[...])

    @pl.when(j == pl.num_programs(1) - 1)
    def _():
        out = h_ref[...] + 0.5 * acc_ref[...]
        if final_norm:
            out = _rms_norm(out, fg_ref[...])
        o_ref[...] = out


def _ffn(h, g, wg, wu, wd, fg, *, final_norm, tm=512, tf=512):
    t, d = h.shape
    dff = wg.shape[1]
    grid = (t // tm, dff // tf)
    return pl.pallas_call(
        functools.partial(_ffn_kernel, final_norm=final_norm),
        grid=grid,
        in_specs=[
            pl.BlockSpec((tm, d), lambda i, j: (i, 0)),
            pl.BlockSpec((1, d), lambda i, j: (0, 0)),
            pl.BlockSpec((d, tf), lambda i, j: (0, j)),
            pl.BlockSpec((d, tf), lambda i, j: (0, j)),
            pl.BlockSpec((tf, d), lambda i, j: (j, 0)),
            pl.BlockSpec((1, d), lambda i, j: (0, 0)),
        ],
        out_specs=pl.BlockSpec((tm, d), lambda i, j: (i, 0)),
        out_shape=jax.ShapeDtypeStruct((t, d), F32),
        scratch_shapes=[pltpu.VMEM((tm, d), BF16), pltpu.VMEM((tm, d), F32)],
        compiler_params=pltpu.CompilerParams(
            dimension_semantics=("parallel", "arbitrary"), vmem_limit_bytes=VMEM_LIMIT),
        name="ffn",
    )(h, g, wg, wu, wd, fg)


def _proj_kernel(h_ref, g_ref, w_ref, o_ref, xn_ref):
    @pl.when(pl.program_id(1) == 0)
    def _():
        xn_ref[...] = _rms_norm(h_ref[...], g_ref[...]).astype(BF16)

    o_ref[...] = _dot(xn_ref[...], w_ref[...])


def _proj(h, g, w, *, tm=512, tn=512):
    t, d = h.shape
    n = w.shape[1]
    return pl.pallas_call(
        _proj_kernel,
        grid=(t // tm, n // tn),
        in_specs=[
            pl.BlockSpec((tm, d), lambda i, j: (i, 0)),
            pl.BlockSpec((1, d), lambda i, j: (0, 0)),
            pl.BlockSpec((d, tn), lambda i, j: (0, j)),
        ],
        out_specs=pl.BlockSpec((tm, tn), lambda i, j: (i, j)),
        out_shape=jax.ShapeDtypeStruct((t, n), F32),
        scratch_shapes=[pltpu.VMEM((tm, d), BF16)],
        compiler_params=pltpu.CompilerParams(
            dimension_semantics=("parallel", "arbitrary"), vmem_limit_bytes=VMEM_LIMIT),
        name="proj",
    )(h, g, w)


def _outproj_kernel(h_ref, ya_ref, yb_ref, w_ref, o_ref):
    ka = ya_ref.shape[1]
    o_ref[...] = (h_ref[...] + _dot(ya_ref[...], w_ref[:ka, :]) + _dot(yb_ref[...], w_ref[ka:, :]))


def _outproj(h, ya, yb, w, *, tm=512):
    t, d = h.shape
    ka, kb = ya.shape[1], yb.shape[1]
    return pl.pallas_call(
        _outproj_kernel,
        grid=(t // tm,),
        in_specs=[
            pl.BlockSpec((tm, d), lambda i: (i, 0)),
            pl.BlockSpec((tm, ka), lambda i: (i, 0)),
            pl.BlockSpec((tm, kb), lambda i: (i, 0)),
            pl.BlockSpec((ka + kb, d), lambda i: (0, 0)),
        ],
        out_specs=pl.BlockSpec((tm, d), lambda i: (i, 0)),
        out_shape=jax.ShapeDtypeStruct((t, d), F32),
        compiler_params=pltpu.CompilerParams(
            dimension_semantics=("parallel",), vmem_limit_bytes=VMEM_LIMIT),
        name="outproj",
    )(h, ya, yb, w)


def _split3(x):
    hi = x.astype(BF16)
    r1 = x - hi.astype(F32)
    mid = r1.astype(BF16)
    lo = (r1 - mid.astype(F32)).astype(BF16)
    return hi, mid, lo


def _head_sum(x, ones_bd):
    hi = x.astype(BF16)
    lo = (x - hi.astype(F32)).astype(BF16)
    outs = []
    for s in range(x.shape[1] // LANES):
        sl = slice(s * LANES, (s + 1) * LANES)
        outs.append(_dot(hi[:, sl], ones_bd) + _dot(lo[:, sl], ones_bd))
    return jnp.concatenate(outs, axis=1)


def _shift_rows(x, carry_row):
    rolled = pltpu.roll(x, 1, 0)
    row = lax.broadcasted_iota(jnp.int32, x.shape, 0)
    return jnp.where(row == 0, carry_row, rolled)


def _rwkv_kernel(r_ref, k_ref, v_ref, s_ref, vec_ref, mus_ref, w2_ref, a2_ref, g2_ref,
                 o_ref,
                 prev_ref, hs_ref, ab_ref, bt_ref, kt_ref, rb_ref, vv_ref, bh_ref, kh_ref,
                 pc_ref, y_ref):
    tb, width = r_ref.shape
    n_chunks = tb // CHUNK
    n_pairs = width // LANES

    @pl.when(pl.program_id(1) == 0)
    def _():
        prev_ref[...] = jnp.zeros_like(prev_ref)
        hs_ref[...] = jnp.zeros_like(hs_ref)

    vec = vec_ref[...]
    w0, a0, k_k, k_a, r_k, ln_w, ln_b = (vec[i:i + 1, :] for i in range(7))
    mu_r, mu_k, mu_v = (vec[i:i + 1, :] for i in range(7, 10))

    def token_shift(ref, mu, off):
        x = ref[...]
        n = x.shape[1]
        prev = _shift_rows(x, prev_ref[0:1, off:off + n])
        prev_ref[0:1, off:off + n] = x[tb - 1:tb, :]
        return x + (prev - x) * mu

    r = token_shift(r_ref, mu_r, 0)
    k = token_shift(k_ref, mu_k, width)
    v = token_shift(v_ref, mu_v, 2 * width)
    sm = token_shift(s_ref, mus_ref[...], 3 * width)

    s1 = sm[:, 0:LANES]
    s2 = sm[:, LANES:LANES + g2_ref.shape[0]]
    w_log = -_softplus(-(w0 + _dot(jnp.tanh(s1).astype(BF16), w2_ref[...]))) - 0.5
    a = _sigmoid(a0 + _dot(s1.astype(BF16), a2_ref[...]))
    g = _dot(_sigmoid(s2).astype(BF16), g2_ref[...])
    ld = -jnp.exp(w_log)

    lane = lax.broadcasted_iota(jnp.int32, (LANES, LANES), 1)
    sub = lax.broadcasted_iota(jnp.int32, (LANES, LANES), 0)
    ones_bd = jnp.where((lane // HEAD_DIM) == (sub // HEAD_DIM), 1.0, 0.0).astype(BF16)

    kk = k * k_k
    kk = kk * lax.rsqrt(jnp.maximum(_head_sum(kk * kk, ones_bd), 1e-24))
    k = k * (1.0 + (a - 1.0) * k_a)

    rr = lax.broadcasted_iota(jnp.int32, (tb, tb), 0)
    cc = lax.broadcasted_iota(jnp.int32, (tb, tb), 1)
    lmat = jnp.where(((rr // CHUNK) == (cc // CHUNK)) & (cc <= rr), 1.0, 0.0).astype(BF16)
    hi, mid, lo = _split3(ld)
    cum = _dot(lmat, hi) + _dot(lmat, mid) + _dot(lmat, lo)
    lastm = jnp.where(((rr // CHUNK) == (cc // CHUNK)), 1.0, 0.0).astype(BF16)
    tot = _dot(lastm, hi) + _dot(lastm, mid) + _dot(lastm, lo)

    kb = kk * a
    q = jnp.exp(-cum)
    e_end = jnp.exp(tot - cum)
    ab_ref[...] = (-kk * jnp.exp(cum - ld)).astype(BF16)
    bt_ref[...] = (kb * q).astype(BF16)
    kt_ref[...] = (k * q).astype(BF16)
    rb_ref[...] = (r * jnp.exp(cum)).astype(BF16)
    vv_ref[...] = v.astype(BF16)
    bh_ref[...] = (kb * e_end).astype(BF16)
    kh_ref[...] = (k * e_end).astype(BF16)
    pc_ref[...] = jnp.exp(tot)

    lane_h = lax.broadcasted_iota(jnp.int32, (CHUNK, LANES), 1) < HEAD_DIM
    strict = lane < sub
    incl = lane <= sub
    eye = jnp.where(lane == sub, 1.0, 0.0).astype(F32)

    def stack(x):
        zero = jnp.zeros_like(x)
        return jnp.concatenate([jnp.where(lane_h, x, zero), jnp.where(lane_h, zero, x)], axis=0)

    def chunk_body(ci, carry):
        rows = pl.ds(pl.multiple_of(ci * CHUNK, CHUNK), CHUNK)
        for pi in range(n_pairs):
            cols = slice(pi * LANES, (pi + 1) * LANES)
            xa = stack(ab_ref[rows, cols])
            xr = stack(rb_ref[rows, cols])
            yb = stack(bt_ref[rows, cols])
            yk = stack(kt_ref[rows, cols])
            vs = stack(vv_ref[rows, cols])
            bh = stack(bh_ref[rows, cols])
            kh = stack(kh_ref[rows, cols])
            hs = hs_ref[pi]
            hsb = hs.astype(BF16)

            a_ab = jnp.where(strict, _dot_nt(xa, yb), 0.0)
            a_ak = jnp.where(strict, _dot_nt(xa, yk), 0.0)
            a_rb = jnp.where(incl, _dot_nt(xr, yb), 0.0)
            a_rk = jnp.where(incl, _dot_nt(xr, yk), 0.0)

            pw = a_ab.astype(BF16)
            tinv = eye + a_ab
            n_sq = CHUNK.bit_length() - 2
            for _ in range(n_sq):
                pw_f = _dot(pw, pw)
                pw = pw_f.astype(BF16)
                tinv = tinv + _dot(tinv.astype(BF16), pw)

            rhs = _dot_nt(xa, hsb) + _dot(a_ak.astype(BF16), vs)
            u = _dot(tinv.astype(BF16), rhs.astype(BF16))
            ub = u.astype(BF16)
            y_st = _dot_nt(xr, hsb) + _dot(a_rb.astype(BF16), ub) + _dot(a_rk.astype(BF16), vs)
            pc = pc_ref[pl.ds(pl.multiple_of(ci * CHUNK, CHUNK), 1), cols]
            hs_ref[pi] = hs * pc + _dot_tn(ub, bh) + _dot_tn(vs, kh)
            y_ref[rows, cols] = y_st[0:CHUNK, :] + y_st[CHUNK:2 * CHUNK, :]
        return carry

    lax.fori_loop(0, n_chunks, chunk_body, 0)

    y = y_ref[...]
    inv_n = 1.0 / HEAD_DIM
    mean = _head_sum(y, ones_bd) * inv_n
    yc = y - mean
    var = _head_sum(yc * yc, ones_bd) * inv_n
    yn = yc * lax.rsqrt(var + GN_EPS) * ln_w + ln_b
    bonus = _head_sum(r * k * r_k, ones_bd) * v
    o_ref[...] = ((yn + bonus) * g).astype(o_ref.dtype)


def _rwkv(p, vec, mus, w2p, a2p, g2p, *, width, small_col, small_w, tb=256):
    b, s, _ = p.shape
    n_pairs = width // LANES
    col = lambda c: pl.BlockSpec((None, tb, width), lambda bi, ti, c=c: (bi, ti, c))
    full = lambda a: pl.BlockSpec(a.shape, lambda bi, ti: (0,) * a.ndim)
    return pl.pallas_call(
        _rwkv_kernel,
        grid=(b, s // tb),
        in_specs=[col(0), col(1), col(2),
                  pl.BlockSpec((None, tb, small_w), lambda bi, ti: (bi, ti, small_col)),
                  full(vec), full(mus), full(w2p), full(a2p), full(g2p)],
        out_specs=pl.BlockSpec((None, tb, width), lambda bi, ti: (bi, ti, 0)),
        out_shape=jax.ShapeDtypeStruct((b, s, width), BF16),
        scratch_shapes=[
            pltpu.VMEM((8, 3 * width + small_w), F32),
            pltpu.VMEM((n_pairs, LANES, LANES), F32),
            pltpu.VMEM((tb, width), BF16), pltpu.VMEM((tb, width), BF16),
            pltpu.VMEM((tb, width), BF16), pltpu.VMEM((tb, width), BF16),
            pltpu.VMEM((tb, width), BF16), pltpu.VMEM((tb, width), BF16),
            pltpu.VMEM((tb, width), BF16),
            pltpu.VMEM((tb, width), F32),
            pltpu.VMEM((tb, width), F32),
        ],
        compiler_params=pltpu.CompilerParams(
            dimension_semantics=("arbitrary", "arbitrary"), vmem_limit_bytes=VMEM_LIMIT),
        name="rwkv",
    )(p, p, p, p, vec, mus, w2p, a2p, g2p)


def _lru_kernel(x_ref, gate_ref, vec_ref, w_ref, o_ref, ext_ref, a_ref, u_ref, h_ref, hc_ref):
    ts, width = x_ref.shape
    pad = 8

    @pl.when(pl.program_id(1) == 0)
    def _():
        ext_ref[0:pad, :] = jnp.zeros((pad, width), F32)
        hc_ref[...] = jnp.zeros_like(hc_ref)

    vec = vec_ref[...]
    conv_b, b_a, b_x, lam, norm_g = (vec[i:i + 1, :] for i in range(CONV_WIDTH, CONV_WIDTH + 5))

    ext_ref[pad:pad + ts, :] = x_ref[...]
    xc = conv_b + vec[CONV_WIDTH - 1:CONV_WIDTH, :] * x_ref[...]
    for j in range(CONV_WIDTH - 1):
        xc = xc + vec[j:j + 1, :] * ext_ref[pl.ds(pad - (CONV_WIDTH - 1) + j, ts), :]
    ext_ref[0:pad, :] = ext_ref[ts:ts + pad, :]

    xcb = xc.astype(BF16)
    ra, ix = [], []
    for s in range(width // LANES):
        z = _dot(xcb[:, s * LANES:(s + 1) * LANES], w_ref[s])
        ra.append(z[:, :LANES])
        ix.append(z[:, LANES:])
    r = _sigmoid(jnp.concatenate(ra, axis=1) + b_a)
    i = _sigmoid(jnp.concatenate(ix, axis=1) + b_x)
    log_a = (-LRU_C) * r * _softplus(-lam)
    a = jnp.exp(log_a)
    u = jnp.sqrt(-jnp.tanh(log_a) * (a * a + 1.0)) * (i * xc)
    a_ref[...] = a
    u_ref[...] = u

    row = lax.broadcasted_iota(jnp.int32, (8, width), 0)

    def group(gi, hc):
        rows = pl.ds(pl.multiple_of(gi * 8, 8), 8)
        aa = a_ref[rows, :]
        uu = u_ref[rows, :]
        for sh in (1, 2, 4):
            ok = row >= sh
            a_sh = pltpu.roll(aa, sh, 0)
            u_sh = pltpu.roll(uu, sh, 0)
            uu = jnp.where(ok, aa * u_sh + uu, uu)
            aa = jnp.where(ok, aa * a_sh, aa)
        hh = aa * hc + uu
        h_ref[rows, :] = hh
        return jnp.broadcast_to(hh[7:8, :], (8, width))

    hc_ref[...] = lax.fori_loop(0, ts // 8, group, hc_ref[...])

    gate = gate_ref[...]
    gelu = 0.5 * gate * (1.0 + jnp.tanh(0.7978845608028654 * (gate + 0.044715 * gate * gate * gate)))
    y = h_ref[...] * gelu
    o_ref[...] = _rms_norm(y, norm_g).astype(o_ref.dtype)


def _lru(p, vec, w_bd, *, width, x_col, gate_col, ts=256):
    b, s, _ = p.shape
    full = lambda a: pl.BlockSpec(a.shape, lambda bi, ti: (0,) * a.ndim)
    return pl.pallas_call(
        _lru_kernel,
        grid=(b, s // ts),
        in_specs=[pl.BlockSpec((None, ts, width), lambda bi, ti: (bi, ti, x_col)),
                  pl.BlockSpec((None, ts, width), lambda bi, ti: (bi, ti, gate_col)),
                  full(vec), full(w_bd)],
        out_specs=pl.BlockSpec((None, ts, width), lambda bi, ti: (bi, ti, 0)),
        out_shape=jax.ShapeDtypeStruct((b, s, width), BF16),
        scratch_shapes=[
            pltpu.VMEM((ts + 8, width), F32),
            pltpu.VMEM((ts, width), F32), pltpu.VMEM((ts, width), F32), pltpu.VMEM((ts, width), F32),
            pltpu.VMEM((8, width), F32),
        ],
        compiler_params=pltpu.CompilerParams(
            dimension_semantics=("arbitrary", "arbitrary"), vmem_limit_bytes=VMEM_LIMIT),
        name="lru",
    )(p, p, vec, w_bd)


def _pad_rows(w, rows, at):
    out = jnp.zeros((rows, w.shape[1]), w.dtype)
    return out.at[at:at + w.shape[0]].set(w)


def _block_diag_pairs(wa, wx):
    def bd(w):
        h, n, _ = w.shape
        w = w.reshape(h // PAIR, PAIR, n, n)
        z = jnp.zeros_like(w[:, 0])
        top = jnp.concatenate([w[:, 0], z], axis=2)
        bot = jnp.concatenate([z, w[:, 1]], axis=2)
        return jnp.concatenate([top, bot], axis=1)
    return jnp.concatenate([bd(wa), bd(wx)], axis=2)


def kernel(x, ffn1_norm, ffn1_w_gate, ffn1_w_up, ffn1_w_down, mix_norm, w_in, rwkv_mu, rwkv_w0, rwkv_w2, rwkv_a0, rwkv_a2, rwkv_g2, rwkv_k_k, rwkv_k_a, rwkv_r_k, rwkv_ln_w, rwkv_ln_b, lru_conv_w, lru_conv_b, lru_wa, lru_ba, lru_wx, lru_bx, lru_lam, lru_norm, w_out, ffn2_norm, ffn2_w_gate, ffn2_w_up, ffn2_w_down, final_norm):
    bsz, seq, d = x.shape
    depth = w_in.shape[0]
    wr = rwkv_w0.shape[1]
    wl = lru_lam.shape[1]
    n_w, n_a, n_g = rwkv_w2.shape[1], rwkv_a2.shape[1], rwkv_g2.shape[1]
    assert wr == wl and wr % LANES == 0 and n_w + n_a == LANES and lru_wa.shape[2] == HEAD_DIM
    small_w = 4 * LANES
    g_pad = 2 * LANES
    assert n_g <= g_pad and (3 * wr) % small_w == 0

    row = lambda v: v.reshape(1, -1).astype(F32)
    h = x.reshape(bsz * seq, d)
    for l in range(depth):
        h = _ffn(h, row(ffn1_norm[l]), ffn1_w_gate[l].astype(BF16), ffn1_w_up[l].astype(BF16),
                 ffn1_w_down[l].astype(BF16), row(final_norm), final_norm=False)

        o_r, o_w = 0, wr
        o_k = o_w + n_w
        o_v = o_k + wr
        o_a = o_v + wr
        o_g = o_a + n_a
        o_l = o_g + n_g
        w = w_in[l]
        zpad = jnp.zeros((d, small_w - n_w - n_a - n_g), w.dtype)
        w_re = jnp.concatenate([w[:, o_r:o_r + wr], w[:, o_k:o_k + wr], w[:, o_v:o_v + wr],
                                w[:, o_l:o_l + 2 * wl],
                                w[:, o_w:o_w + n_w], w[:, o_a:o_a + n_a], w[:, o_g:o_g + n_g], zpad],
                               axis=1).astype(BF16)
        mu = rwkv_mu[l]
        mus = jnp.concatenate([mu[o_w:o_w + n_w], mu[o_a:o_a + n_a], mu[o_g:o_g + n_g],
                               jnp.zeros((small_w - n_w - n_a - n_g,), F32)]).reshape(1, -1)
        vec = jnp.stack([rwkv_w0[l], rwkv_a0[l], rwkv_k_k[l], rwkv_k_a[l], rwkv_r_k[l].reshape(-1),
                         rwkv_ln_w[l], rwkv_ln_b[l],
                         mu[o_r:o_r + wr], mu[o_k:o_k + wr], mu[o_v:o_v + wr]]
                        + [jnp.zeros((wr,), F32)] * 6).astype(F32)
        w2p = _pad_rows(rwkv_w2[l], LANES, 0).astype(BF16)
        a2p = _pad_rows(rwkv_a2[l], LANES, n_w).astype(BF16)
        g2p = _pad_rows(rwkv_g2[l], g_pad, 0).astype(BF16)

        lvec = jnp.concatenate([lru_conv_w[l], jnp.stack([lru_conv_b[l], lru_ba[l], lru_bx[l],
                                                          lru_lam[l], lru_norm[l]]),
                                jnp.zeros((16 - CONV_WIDTH - 5, wl), F32)], axis=0).astype(F32)
        w_bd = _block_diag_pairs(lru_wa[l], lru_wx[l]).astype(BF16)

        p = _proj(h, row(mix_norm[l]), w_re).reshape(bsz, seq, -1)
        y_r = _rwkv(p, vec, mus, w2p, a2p, g2p, width=wr, small_col=(3 * wr + 2 * wl) // small_w,
                    small_w=small_w)
        y_l = _lru(p, lvec, w_bd, width=wl, x_col=3, gate_col=4)
        h = _outproj(h, y_r.reshape(bsz * seq, wr), y_l.reshape(bsz * seq, wl), w_out[l].astype(BF16))
        h = _ffn(h, row(ffn2_norm[l]), ffn2_w_gate[l].astype(BF16), ffn2_w_up[l].astype(BF16),
                 ffn2_w_down[l].astype(BF16), row(final_norm), final_norm=(l == depth - 1))
    return h.reshape(bsz, seq, d)
```

```python
import functools

import jax
import jax.numpy as jnp
from jax import lax
from jax.experimental import pallas as pl
from jax.experimental.pallas import tpu as pltpu

F32 = jnp.float32
BF16 = jnp.bfloat16

HEAD_DIM = 64
CONV_WIDTH = 4
LRU_C = 8.0
NORM_EPS = 1e-6
GN_EPS = 64e-5

LANES = 128
CHUNK = 64
PAIR = LANES // HEAD_DIM
SLAB = 256

VMEM_LIMIT = 56 * 1024 * 1024


def _dot(a, b):
    return jnp.dot(a, b, preferred_element_type=F32)


def _dot_nt(a, b):
    return lax.dot_general(a, b, (((1,), (1,)), ((), ())), preferred_element_type=F32)


def _dot_tn(a, b):
    return lax.dot_general(a, b, (((0,), (0,)), ((), ())), preferred_element_type=F32)


def _sigmoid(x):
    return 0.5 * (jnp.tanh(0.5 * x) + 1.0)


def _softplus(x):
    return jnp.maximum(x, 0.0) + jnp.log(1.0 + jnp.exp(-jnp.abs(x)))


def _rms_norm(x, g):
    ms = jnp.mean(x * x, axis=-1, keepdims=True)
    return x * lax.rsqrt(ms + NORM_EPS) * g


def _ffn_kernel(h_ref, g_ref, wg_ref, wu_ref, wd_ref, fg_ref, o_ref, xn_ref, acc_ref, *, final_norm):
    j = pl.program_id(1)

    @pl.when(j == 0)
    def _():
        xn_ref[...] = _rms_norm(h_ref[...], g_ref[...]).astype(BF16)
        acc_ref[...] = jnp.zeros_like(acc_ref)

    xn = xn_ref[...]
    gate = _dot(xn, wg_ref[...])
    up = _dot(xn, wu_ref[...])
    act = (gate * _sigmoid(gate) * up).astype(BF16)
    acc_ref[...] += _dot(act, wd_ref[...])

    @pl.when(j == pl.num_programs(1) - 1)
    def _():
        out = h_ref[...] + 0.5 * acc_ref[...]
        if final_norm:
            out = _rms_norm(out, fg_ref[...])
        o_ref[...] = out


def _ffn(h, g, wg, wu, wd, fg, *, final_norm, tm=512, tf=512):
    t, d = h.shape
    dff = wg.shape[1]
    grid = (t // tm, dff // tf)
    return pl.pallas_call(
        functools.partial(_ffn_kernel, final_norm=final_norm),
        grid=grid,
        in_specs=[
            pl.BlockSpec((tm, d), lambda i, j: (i, 0)),
            pl.BlockSpec((1, d), lambda i, j: (0, 0)),
            pl.BlockSpec((d, tf), lambda i, j: (0, j)),
            pl.BlockSpec((d, tf), lambda i, j: (0, j)),
            pl.BlockSpec((tf, d), lambda i, j: (j, 0)),
            pl.BlockSpec((1, d), lambda i, j: (0, 0)),
        ],
        out_specs=pl.BlockSpec((tm, d), lambda i, j: (i, 0)),
        out_shape=jax.ShapeDtypeStruct((t, d), F32),
        scratch_shapes=[pltpu.VMEM((tm, d), BF16), pltpu.VMEM((tm, d), F32)],
        compiler_params=pltpu.CompilerParams(
            dimension_semantics=("parallel", "arbitrary"), vmem_limit_bytes=VMEM_LIMIT),
        name="ffn",
    )(h, g, wg, wu, wd, fg)


def _proj_kernel(h_ref, g_ref, w_ref, o_ref, xn_ref):
    @pl.when(pl.program_id(1) == 0)
    def _():
        xn_ref[...] = _rms_norm(h_ref[...], g_ref[...]).astype(BF16)

    o_ref[...] = _dot(xn_ref[...], w_ref[...])


def _proj(h, g, w, *, tm=512, tn=512):
    t, d = h.shape
    n = w.shape[1]
    return pl.pallas_call(
        _proj_kernel,
        grid=(t // tm, n // tn),
        in_specs=[
            pl.BlockSpec((tm, d), lambda i, j: (i, 0)),
            pl.BlockSpec((1, d), lambda i, j: (0, 0)),
            pl.BlockSpec((d, tn), lambda i, j: (0, j)),
        ],
        out_specs=pl.BlockSpec((tm, tn), lambda i, j: (i, j)),
        out_shape=jax.ShapeDtypeStruct((t, n), F32),
        scratch_shapes=[pltpu.VMEM((tm, d), BF16)],
        compiler_params=pltpu.CompilerParams(
            dimension_semantics=("parallel", "arbitrary"), vmem_limit_bytes=VMEM_LIMIT),
        name="proj",
    )(h, g, w)


def _outproj_kernel(h_ref, ya_ref, yb_ref, w_ref, o_ref):
    ka = ya_ref.shape[1]
    o_ref[...] = (h_ref[...] + _dot(ya_ref[...], w_ref[:ka, :]) + _dot(yb_ref[...], w_ref[ka:, :]))


def _outproj(h, ya, yb, w, *, tm=512):
    t, d = h.shape
    ka, kb = ya.shape[1], yb.shape[1]
    return pl.pallas_call(
        _outproj_kernel,
        grid=(t // tm,),
        in_specs=[
            pl.BlockSpec((tm, d), lambda i: (i, 0)),
            pl.BlockSpec((tm, ka), lambda i: (i, 0)),
            pl.BlockSpec((tm, kb), lambda i: (i, 0)),
            pl.BlockSpec((ka + kb, d), lambda i: (0, 0)),
        ],
        out_specs=pl.BlockSpec((tm, d), lambda i: (i, 0)),
        out_shape=jax.ShapeDtypeStruct((t, d), F32),
        compiler_params=pltpu.CompilerParams(
            dimension_semantics=("parallel",), vmem_limit_bytes=VMEM_LIMIT),
        name="outproj",
    )(h, ya, yb, w)


def _split3(x):
    hi = x.astype(BF16)
    r1 = x - hi.astype(F32)
    mid = r1.astype(BF16)
    lo = (r1 - mid.astype(F32)).astype(BF16)
    return hi, mid, lo


def _head_sum(x, ones_bd):
    hi = x.astype(BF16)
    lo = (x - hi.astype(F32)).astype(BF16)
    outs = []
    for s in range(x.shape[1] // LANES):
        sl = slice(s * LANES, (s + 1) * LANES)
        outs.append(_dot(hi[:, sl], ones_bd) + _dot(lo[:, sl], ones_bd))
    return jnp.concatenate(outs, axis=1)


def _shift_rows(x, carry_row):
    rolled = pltpu.roll(x, 1, 0)
    row = lax.broadcasted_iota(jnp.int32, x.shape, 0)
    return jnp.where(row == 0, carry_row, rolled)


def _rwkv_kernel(r_ref, k_ref, v_ref, s_ref, vec_ref, mus_ref, w2_ref, a2_ref, g2_ref,
                 o_ref,
                 prev_ref, hs_ref, ab_ref, bt_ref, kt_ref, rb_ref, vv_ref, bh_ref, kh_ref,
                 pc_ref, y_ref, pw_ref, tf_ref, aak_ref, arb_ref, ark_ref):
    tb, width = r_ref.shape
    n_chunks = tb // CHUNK

    @pl.when(pl.program_id(1) == 0)
    def _():
        prev_ref[...] = jnp.zeros_like(prev_ref)
        hs_ref[...] = jnp.zeros_like(hs_ref)

    vec = vec_ref[...]
    w0, a0, k_k, k_a, r_k, ln_w, ln_b = (vec[i:i + 1, :] for i in range(7))
    mu_r, mu_k, mu_v = (vec[i:i + 1, :] for i in range(7, 10))

    def token_shift(ref, mu, off):
        x = ref[...]
        n = x.shape[1]
        prev = _shift_rows(x, prev_ref[0:1, off:off + n])
        prev_ref[0:1, off:off + n] = x[tb - 1:tb, :]
        return x + (prev - x) * mu

    r = token_shift(r_ref, mu_r, 0)
    k = token_shift(k_ref, mu_k, width)
    v = token_shift(v_ref, mu_v, 2 * width)
    sm = token_shift(s_ref, mus_ref[...], 3 * width)

    s1 = sm[:, 0:LANES]
    s2 = sm[:, LANES:LANES + g2_ref.shape[0]]
    w_log = -_softplus(-(w0 + _dot(jnp.tanh(s1).astype(BF16), w2_ref[...]))) - 0.5
    a = _sigmoid(a0 + _dot(s1.astype(BF16), a2_ref[...]))
    g = _dot(_sigmoid(s2).astype(BF16), g2_ref[...])
    ld = -jnp.exp(w_log)

    lane = lax.broadcasted_iota(jnp.int32, (LANES, LANES), 1)
    sub = lax.broadcasted_iota(jnp.int32, (LANES, LANES), 0)
    ones_bd = jnp.where((lane // HEAD_DIM) == (sub // HEAD_DIM), 1.0, 0.0).astype(BF16)

    kk = k * k_k
    kk = kk * lax.rsqrt(jnp.maximum(_head_sum(kk * kk, ones_bd), 1e-24))
    k = k * (1.0 + (a - 1.0) * k_a)

    rr = lax.broadcasted_iota(jnp.int32, (tb, tb), 0)
    cc = lax.broadcasted_iota(jnp.int32, (tb, tb), 1)
    lmat = jnp.where(((rr // CHUNK) == (cc // CHUNK)) & (cc <= rr), 1.0, 0.0).astype(BF16)
    hi, mid, lo = _split3(ld)
    cum = _dot(lmat, hi) + _dot(lmat, mid) + _dot(lmat, lo)
    lastm = jnp.where(((rr // CHUNK) == (cc // CHUNK)), 1.0, 0.0).astype(BF16)
    tot = _dot(lastm, hi) + _dot(lastm, mid) + _dot(lastm, lo)

    kb = kk * a
    q = jnp.exp(-cum)
    e_end = jnp.exp(tot - cum)
    ab_ref[...] = (-kk * jnp.exp(cum - ld)).astype(BF16)
    bt_ref[...] = (kb * q).astype(BF16)
    kt_ref[...] = (k * q).astype(BF16)
    rb_ref[...] = (r * jnp.exp(cum)).astype(BF16)
    vv_ref[...] = v.astype(BF16)
    bh_ref[...] = (kb * e_end).astype(BF16)
    kh_ref[...] = (k * e_end).astype(BF16)
    pc_ref[...] = jnp.exp(tot)

    n_slabs = width // SLAB
    lane_s = lax.broadcasted_iota(jnp.int32, (SLAB, SLAB), 1)
    sub_s = lax.broadcasted_iota(jnp.int32, (SLAB, SLAB), 0)
    strict = lane_s < sub_s
    incl = lane_s <= sub_s
    eye = jnp.where(lane_s == sub_s, 1.0, 0.0).astype(F32)
    head_of_lane = lax.broadcasted_iota(jnp.int32, (CHUNK, SLAB), 1) // HEAD_DIM

    def stack(x):
        zero = jnp.zeros_like(x)
        return jnp.concatenate([jnp.where(head_of_lane == hd, x, zero) for hd in range(SLAB // HEAD_DIM)],
                               axis=0)

    n_sq = CHUNK.bit_length() - 2

    def chunk_body(ci, carry):
        row0 = pl.multiple_of(ci * CHUNK, CHUNK)
        rows = pl.ds(row0, CHUNK)
        cols = [slice(q * SLAB, (q + 1) * SLAB) for q in range(n_slabs)]
        slabs = range(n_slabs)

        for q in slabs:
            xx = jnp.concatenate([stack(ab_ref[rows, cols[q]]), stack(rb_ref[rows, cols[q]])], axis=0)
            gb = _dot_nt(xx, stack(bt_ref[rows, cols[q]]))
            gk = _dot_nt(xx, stack(kt_ref[rows, cols[q]]))
            a_ab = jnp.where(strict, gb[:SLAB], 0.0)
            pw_ref[0, q] = a_ab.astype(BF16)
            tf_ref[q] = eye + a_ab
            aak_ref[q] = jnp.where(strict, gk[:SLAB], 0.0).astype(BF16)
            arb_ref[q] = jnp.where(incl, gb[SLAB:], 0.0).astype(BF16)
            ark_ref[q] = jnp.where(incl, gk[SLAB:], 0.0).astype(BF16)
        for s in range(n_sq):
            src, dst = s % 2, (s + 1) % 2
            for q in slabs:
                pw_ref[dst, q] = _dot(pw_ref[src, q], pw_ref[src, q]).astype(BF16)
            for q in slabs:
                tf_ref[q] = tf_ref[q] + _dot(tf_ref[q].astype(BF16), pw_ref[dst, q])

        vs = [stack(vv_ref[rows, cols[q]]) for q in slabs]
        xh = [_dot_nt(jnp.concatenate([stack(ab_ref[rows, cols[q]]), stack(rb_ref[rows, cols[q]])], axis=0),
                      hs_ref[q].astype(BF16)) for q in slabs]
        rhs = [(xh[q][:SLAB] + _dot(aak_ref[q], vs[q])).astype(BF16) for q in slabs]
        u = [_dot(tf_ref[q].astype(BF16), rhs[q]).astype(BF16) for q in slabs]
        for q in slabs:
            y_st = xh[q][SLAB:] + _dot(arb_ref[q], u[q]) + _dot(ark_ref[q], vs[q])
            y = y_st[0:CHUNK]
            for hd in range(1, SLAB // HEAD_DIM):
                y = y + y_st[hd * CHUNK:(hd + 1) * CHUNK]
            y_ref[rows, cols[q]] = y
        for q in slabs:
            pc = pc_ref[pl.ds(row0, 1), cols[q]]
            hs_ref[q] = (hs_ref[q] * pc + _dot_tn(u[q], stack(bh_ref[rows, cols[q]]))
                         + _dot_tn(vs[q], stack(kh_ref[rows, cols[q]])))
        return carry

    lax.fori_loop(0, n_chunks, chunk_body, 0)

    y = y_ref[...]
    inv_n = 1.0 / HEAD_DIM
    mean = _head_sum(y, ones_bd) * inv_n
    yc = y - mean
    var = _head_sum(yc * yc, ones_bd) * inv_n
    yn = yc * lax.rsqrt(var + GN_EPS) * ln_w + ln_b
    bonus = _head_sum(r * k * r_k, ones_bd) * v
    o_ref[...] = ((yn + bonus) * g).astype(o_ref.dtype)


def _rwkv(p, vec, mus, w2p, a2p, g2p, *, width, small_col, small_w, tb=256):
    b, s, _ = p.shape
    n_slabs = width // SLAB
    col = lambda c: pl.BlockSpec((None, tb, width), lambda bi, ti, c=c: (bi, ti, c))
    full = lambda a: pl.BlockSpec(a.shape, lambda bi, ti: (0,) * a.ndim)
    return pl.pallas_call(
        _rwkv_kernel,
        grid=(b, s // tb),
        in_specs=[col(0), col(1), col(2),
                  pl.BlockSpec((None, tb, small_w), lambda bi, ti: (bi, ti, small_col)),
                  full(vec), full(mus), full(w2p), full(a2p), full(g2p)],
        out_specs=pl.BlockSpec((None, tb, width), lambda bi, ti: (bi, ti, 0)),
        out_shape=jax.ShapeDtypeStruct((b, s, width), BF16),
        scratch_shapes=[
            pltpu.VMEM((8, 3 * width + small_w), F32),
            pltpu.VMEM((n_slabs, SLAB, SLAB), F32),
            pltpu.VMEM((tb, width), BF16), pltpu.VMEM((tb, width), BF16),
            pltpu.VMEM((tb, width), BF16), pltpu.VMEM((tb, width), BF16),
            pltpu.VMEM((tb, width), BF16), pltpu.VMEM((tb, width), BF16),
            pltpu.VMEM((tb, width), BF16),
            pltpu.VMEM((tb, width), F32),
            pltpu.VMEM((tb, width), F32),
            pltpu.VMEM((2, n_slabs, SLAB, SLAB), BF16),
            pltpu.VMEM((n_slabs, SLAB, SLAB), F32),
            pltpu.VMEM((n_slabs, SLAB, SLAB), BF16), pltpu.VMEM((n_slabs, SLAB, SLAB), BF16),
            pltpu.VMEM((n_slabs, SLAB, SLAB), BF16),
        ],
        compiler_params=pltpu.CompilerParams(
            dimension_semantics=("arbitrary", "arbitrary"), vmem_limit_bytes=VMEM_LIMIT),
        name="rwkv",
    )(p, p, p, p, vec, mus, w2p, a2p, g2p)


def _lru_kernel(x_ref, gate_ref, vec_ref, w_ref, o_ref, ext_ref, a_ref, u_ref, h_ref, hc_ref):
    ts, width = x_ref.shape
    pad = 8

    @pl.when(pl.program_id(1) == 0)
    def _():
        ext_ref[0:pad, :] = jnp.zeros((pad, width), F32)
        hc_ref[...] = jnp.zeros_like(hc_ref)

    vec = vec_ref[...]
    conv_b, b_a, b_x, lam, norm_g = (vec[i:i + 1, :] for i in range(CONV_WIDTH, CONV_WIDTH + 5))

    ext_ref[pad:pad + ts, :] = x_ref[...]
    xc = conv_b + vec[CONV_WIDTH - 1:CONV_WIDTH, :] * x_ref[...]
    for j in range(CONV_WIDTH - 1):
        xc = xc + vec[j:j + 1, :] * ext_ref[pl.ds(pad - (CONV_WIDTH - 1) + j, ts), :]
    ext_ref[0:pad, :] = ext_ref[ts:ts + pad, :]

    xcb = xc.astype(BF16)
    ra, ix = [], []
    for s in range(width // LANES):
        z = _dot(xcb[:, s * LANES:(s + 1) * LANES], w_ref[s])
        ra.append(z[:, :LANES])
        ix.append(z[:, LANES:])
    r = _sigmoid(jnp.concatenate(ra, axis=1) + b_a)
    i = _sigmoid(jnp.concatenate(ix, axis=1) + b_x)
    log_a = (-LRU_C) * r * _softplus(-lam)
    a = jnp.exp(log_a)
    u = jnp.sqrt(-jnp.tanh(log_a) * (a * a + 1.0)) * (i * xc)
    a_ref[...] = a
    u_ref[...] = u

    row = lax.broadcasted_iota(jnp.int32, (8, width), 0)

    def group(gi, hc):
        rows = pl.ds(pl.multiple_of(gi * 8, 8), 8)
        aa = a_ref[rows, :]
        uu = u_ref[rows, :]
        for sh in (1, 2, 4):
            ok = row >= sh
            a_sh = pltpu.roll(aa, sh, 0)
            u_sh = pltpu.roll(uu, sh, 0)
            uu = jnp.where(ok, aa * u_sh + uu, uu)
            aa = jnp.where(ok, aa * a_sh, aa)
        hh = aa * hc + uu
        h_ref[rows, :] = hh
        return jnp.broadcast_to(hh[7:8, :], (8, width))

    hc_ref[...] = lax.fori_loop(0, ts // 8, group, hc_ref[...])

    gate = gate_ref[...]
    gelu = 0.5 * gate * (1.0 + jnp.tanh(0.7978845608028654 * (gate + 0.044715 * gate * gate * gate)))
    y = h_ref[...] * gelu
    o_ref[...] = _rms_norm(y, norm_g).astype(o_ref.dtype)


def _lru(p, vec, w_bd, *, width, x_col, gate_col, ts=256):
    b, s, _ = p.shape
    full = lambda a: pl.BlockSpec(a.shape, lambda bi, ti: (0,) * a.ndim)
    return pl.pallas_call(
        _lru_kernel,
        grid=(b, s // ts),
        in_specs=[pl.BlockSpec((None, ts, width), lambda bi, ti: (bi, ti, x_col)),
                  pl.BlockSpec((None, ts, width), lambda bi, ti: (bi, ti, gate_col)),
                  full(vec), full(w_bd)],
        out_specs=pl.BlockSpec((None, ts, width), lambda bi, ti: (bi, ti, 0)),
        out_shape=jax.ShapeDtypeStruct((b, s, width), BF16),
        scratch_shapes=[
            pltpu.VMEM((ts + 8, width), F32),
            pltpu.VMEM((ts, width), F32), pltpu.VMEM((ts, width), F32), pltpu.VMEM((ts, width), F32),
            pltpu.VMEM((8, width), F32),
        ],
        compiler_params=pltpu.CompilerParams(
            dimension_semantics=("arbitrary", "arbitrary"), vmem_limit_bytes=VMEM_LIMIT),
        name="lru",
    )(p, p, vec, w_bd)


def _pad_rows(w, rows, at):
    out = jnp.zeros((rows, w.shape[1]), w.dtype)
    return out.at[at:at + w.shape[0]].set(w)


def _block_diag_pairs(wa, wx):
    def bd(w):
        h, n, _ = w.shape
        w = w.reshape(h // PAIR, PAIR, n, n)
        z = jnp.zeros_like(w[:, 0])
        top = jnp.concatenate([w[:, 0], z], axis=2)
        bot = jnp.concatenate([z, w[:, 1]], axis=2)
        return jnp.concatenate([top, bot], axis=1)
    return jnp.concatenate([bd(wa), bd(wx)], axis=2)


def kernel(x, ffn1_norm, ffn1_w_gate, ffn1_w_up, ffn1_w_down, mix_norm, w_in, rwkv_mu, rwkv_w0, rwkv_w2, rwkv_a0, rwkv_a2, rwkv_g2, rwkv_k_k, rwkv_k_a, rwkv_r_k, rwkv_ln_w, rwkv_ln_b, lru_conv_w, lru_conv_b, lru_wa, lru_ba, lru_wx, lru_bx, lru_lam, lru_norm, w_out, ffn2_norm, ffn2_w_gate, ffn2_w_up, ffn2_w_down, final_norm):
    bsz, seq, d = x.shape
    depth = w_in.shape[0]
    wr = rwkv_w0.shape[1]
    wl = lru_lam.shape[1]
    n_w, n_a, n_g = rwkv_w2.shape[1], rwkv_a2.shape[1], rwkv_g2.shape[1]
    assert wr == wl and wr % LANES == 0 and n_w + n_a == LANES and lru_wa.shape[2] == HEAD_DIM
    small_w = 4 * LANES
    g_pad = 2 * LANES
    assert n_g <= g_pad and (3 * wr) % small_w == 0

    row = lambda v: v.reshape(1, -1).astype(F32)
    h = x.reshape(bsz * seq, d)
    for l in range(depth):
        h = _ffn(h, row(ffn1_norm[l]), ffn1_w_gate[l].astype(BF16), ffn1_w_up[l].astype(BF16),
                 ffn1_w_down[l].astype(BF16), row(final_norm), final_norm=False)

        o_r, o_w = 0, wr
        o_k = o_w + n_w
        o_v = o_k + wr
        o_a = o_v + wr
        o_g = o_a + n_a
        o_l = o_g + n_g
        w = w_in[l]
        zpad = jnp.zeros((d, small_w - n_w - n_a - n_g), w.dtype)
        w_re = jnp.concatenate([w[:, o_r:o_r + wr], w[:, o_k:o_k + wr], w[:, o_v:o_v + wr],
                                w[:, o_l:o_l + 2 * wl],
                                w[:, o_w:o_w + n_w], w[:, o_a:o_a + n_a], w[:, o_g:o_g + n_g], zpad],
                               axis=1).astype(BF16)
        mu = rwkv_mu[l]
        mus = jnp.concatenate([mu[o_w:o_w + n_w], mu[o_a:o_a + n_a], mu[o_g:o_g + n_g],
                               jnp.zeros((small_w - n_w - n_a - n_g,), F32)]).reshape(1, -1)
        vec = jnp.stack([rwkv_w0[l], rwkv_a0[l], rwkv_k_k[l], rwkv_k_a[l], rwkv_r_k[l].reshape(-1),
                         rwkv_ln_w[l], rwkv_ln_b[l],
                         mu[o_r:o_r + wr], mu[o_k:o_k + wr], mu[o_v:o_v + wr]]
                        + [jnp.zeros((wr,), F32)] * 6).astype(F32)
        w2p = _pad_rows(rwkv_w2[l], LANES, 0).astype(BF16)
        a2p = _pad_rows(rwkv_a2[l], LANES, n_w).astype(BF16)
        g2p = _pad_rows(rwkv_g2[l], g_pad, 0).astype(BF16)

        lvec = jnp.concatenate([lru_conv_w[l], jnp.stack([lru_conv_b[l], lru_ba[l], lru_bx[l],
                                                          lru_lam[l], lru_norm[l]]),
                                jnp.zeros((16 - CONV_WIDTH - 5, wl), F32)], axis=0).astype(F32)
        w_bd = _block_diag_pairs(lru_wa[l], lru_wx[l]).astype(BF16)

        p = _proj(h, row(mix_norm[l]), w_re).reshape(bsz, seq, -1)
        y_r = _rwkv(p, vec, mus, w2p, a2p, g2p, width=wr, small_col=(3 * wr + 2 * wl) // small_w,
                    small_w=small_w)
        y_l = _lru(p, lvec, w_bd, width=wl, x_col=3, gate_col=4)
        h = _outproj(h, y_r.reshape(bsz * seq, wr), y_l.reshape(bsz * seq, wl), w_out[l].astype(BF16))
        h = _ffn(h, row(ffn2_norm[l]), ffn2_w_gate[l].astype(BF16), ffn2_w_up[l].astype(BF16),
                 ffn2_w_down[l].astype(BF16), row(final_norm), final_norm=(l == depth - 1))
    return h.reshape(bsz, seq, d)
```

```python
import functools

import jax
import jax.numpy as jnp
from jax import lax
from jax.experimental import pallas as pl
from jax.experimental.pallas import tpu as pltpu

F32 = jnp.float32
BF16 = jnp.bfloat16

HEAD_DIM = 64
CONV_WIDTH = 4
LRU_C = 8.0
NORM_EPS = 1e-6
GN_EPS = 64e-5

LANES = 128
CHUNK = 64
PAIR = LANES // HEAD_DIM
SLAB = 256

VMEM_LIMIT = 56 * 1024 * 1024


def _dot(a, b):
    return jnp.dot(a, b, preferred_element_type=F32)


def _dot_nt(a, b):
    return lax.dot_general(a, b, (((1,), (1,)), ((), ())), preferred_element_type=F32)


def _dot_tn(a, b):
    return lax.dot_general(a, b, (((0,), (0,)), ((), ())), preferred_element_type=F32)


def _sigmoid(x):
    return 0.5 * (jnp.tanh(0.5 * x) + 1.0)


def _softplus(x):
    return jnp.maximum(x, 0.0) + jnp.log(1.0 + jnp.exp(-jnp.abs(x)))


def _rms_norm(x, g):
    ms = jnp.mean(x * x, axis=-1, keepdims=True)
    return x * lax.rsqrt(ms + NORM_EPS) * g


def _ffn_kernel(h_ref, g_ref, wg_ref, wu_ref, wd_ref, fg_ref, o_ref, xn_ref, acc_ref, *, final_norm):
    j = pl.program_id(1)

    @pl.when(j == 0)
    def _():
        xn_ref[...] = _rms_norm(h_ref[...], g_ref[...]).astype(BF16)
        acc_ref[...] = jnp.zeros_like(acc_ref)

    xn = xn_ref[...]
    gate = _dot(xn, wg_ref[...])
    up = _dot(xn, wu_ref[...])
    act = (gate * _sigmoid(gate) * up).astype(BF16)
    acc_ref[...] += _dot(act, wd_ref[...])

    @pl.when(j == pl.num_programs(1) - 1)
    def _():
        out = h_ref[...] + 0.5 * acc_ref[...]
        if final_norm:
            out = _rms_norm(out, fg_ref[...])
        o_ref[...] = out


def _ffn(h, g, wg, wu, wd, fg, *, final_norm, tm=512, tf=512):
    t, d = h.shape
    dff = wg.shape[1]
    grid = (t // tm, dff // tf)
    return pl.pallas_call(
        functools.partial(_ffn_kernel, final_norm=final_norm),
        grid=grid,
        in_specs=[
            pl.BlockSpec((tm, d), lambda i, j: (i, 0)),
            pl.BlockSpec((1, d), lambda i, j: (0, 0)),
            pl.BlockSpec((d, tf), lambda i, j: (0, j)),
            pl.BlockSpec((d, tf), lambda i, j: (0, j)),
            pl.BlockSpec((tf, d), lambda i, j: (j, 0)),
            pl.BlockSpec((1, d), lambda i, j: (0, 0)),
        ],
        out_specs=pl.BlockSpec((tm, d), lambda i, j: (i, 0)),
        out_shape=jax.ShapeDtypeStruct((t, d), F32),
        scratch_shapes=[pltpu.VMEM((tm, d), BF16), pltpu.VMEM((tm, d), F32)],
        compiler_params=pltpu.CompilerParams(
            dimension_semantics=("parallel", "arbitrary"), vmem_limit_bytes=VMEM_LIMIT),
        name="ffn",
    )(h, g, wg, wu, wd, fg)


def _proj_kernel(h_ref, g_ref, w_ref, o_ref, xn_ref):
    @pl.when(pl.program_id(1) == 0)
    def _():
        xn_ref[...] = _rms_norm(h_ref[...], g_ref[...]).astype(BF16)

    o_ref[...] = _dot(xn_ref[...], w_ref[...]).astype(o_ref.dtype)


def _proj(h, g, w, *, tm=1024, tn=512):
    t, d = h.shape
    n = w.shape[1]
    return pl.pallas_call(
        _proj_kernel,
        grid=(t // tm, n // tn),
        in_specs=[
            pl.BlockSpec((tm, d), lambda i, j: (i, 0)),
            pl.BlockSpec((1, d), lambda i, j: (0, 0)),
            pl.BlockSpec((d, tn), lambda i, j: (0, j)),
        ],
        out_specs=pl.BlockSpec((tm, tn), lambda i, j: (i, j)),
        out_shape=jax.ShapeDtypeStruct((t, n), BF16),
        scratch_shapes=[pltpu.VMEM((tm, d), BF16)],
        compiler_params=pltpu.CompilerParams(
            dimension_semantics=("parallel", "arbitrary"), vmem_limit_bytes=VMEM_LIMIT),
        name="proj",
    )(h, g, w)


def _outproj_kernel(h_ref, ya_ref, yb_ref, w_ref, o_ref):
    ka = ya_ref.shape[1]
    o_ref[...] = (h_ref[...] + _dot(ya_ref[...], w_ref[:ka, :]) + _dot(yb_ref[...], w_ref[ka:, :]))


def _outproj(h, ya, yb, w, *, tm=512):
    t, d = h.shape
    ka, kb = ya.shape[1], yb.shape[1]
    return pl.pallas_call(
        _outproj_kernel,
        grid=(t // tm,),
        in_specs=[
            pl.BlockSpec((tm, d), lambda i: (i, 0)),
            pl.BlockSpec((tm, ka), lambda i: (i, 0)),
            pl.BlockSpec((tm, kb), lambda i: (i, 0)),
            pl.BlockSpec((ka + kb, d), lambda i: (0, 0)),
        ],
        out_specs=pl.BlockSpec((tm, d), lambda i: (i, 0)),
        out_shape=jax.ShapeDtypeStruct((t, d), F32),
        compiler_params=pltpu.CompilerParams(
            dimension_semantics=("parallel",), vmem_limit_bytes=VMEM_LIMIT),
        name="outproj",
    )(h, ya, yb, w)


def _head_sum(x, ones_bd):
    r, w = x.shape
    n = w // SLAB
    xs = jnp.concatenate([x[:, s * SLAB:(s + 1) * SLAB] for s in range(n)], axis=0)
    hi = xs.astype(BF16)
    lo = (xs - hi.astype(F32)).astype(BF16)
    z = _dot(jnp.concatenate([hi, lo], axis=0), ones_bd)
    z = z[:n * r] + z[n * r:]
    return jnp.concatenate([z[s * r:(s + 1) * r] for s in range(n)], axis=1)


def _shift_rows(x, carry_row):
    rolled = pltpu.roll(x, 1, 0)
    row = lax.broadcasted_iota(jnp.int32, x.shape, 0)
    return jnp.where(row == 0, carry_row, rolled)


def _rwkv_kernel(x0_ref, s0_ref, xn_ref, sn_ref,
                 vec_ref, mus_ref, w2_ref, a2_ref, g2_ref,
                 o_ref,
                 prev_ref, hs_ref, ab_ref, bt_ref, kt_ref, rb_ref, vv_ref, bh_ref, kh_ref,
                 pc_ref, bon_ref, gg_ref, y_ref, pw_ref, tf_ref, aak_ref, arb_ref, ark_ref,
                 *, chunks_per_seq):
    j = pl.program_id(0)
    width = o_ref.shape[1]
    n_slabs = width // SLAB
    heads_per_slab = SLAB // HEAD_DIM

    vec = vec_ref[...]
    w0, a0, k_k, k_a, r_k, ln_w, ln_b, mu_r, mu_k, mu_v = (vec[i:i + 1, :] for i in range(10))
    mu_s = mus_ref[...]

    lane_s = lax.broadcasted_iota(jnp.int32, (SLAB, SLAB), 1)
    sub_s = lax.broadcasted_iota(jnp.int32, (SLAB, SLAB), 0)
    strict = lane_s < sub_s
    incl = lane_s <= sub_s
    eye = jnp.where(lane_s == sub_s, 1.0, 0.0).astype(F32)
    ones_bd = jnp.where((lane_s // HEAD_DIM) == (sub_s // HEAD_DIM), 1.0, 0.0).astype(BF16)
    head_of_lane = lax.broadcasted_iota(jnp.int32, (CHUNK, SLAB), 1) // HEAD_DIM
    row_c = lax.broadcasted_iota(jnp.int32, (CHUNK, SLAB), 0)
    cols = [slice(q * SLAB, (q + 1) * SLAB) for q in range(n_slabs)]
    slabs = range(n_slabs)
    n_sq = CHUNK.bit_length() - 2
    halves = (slice(0, CHUNK), slice(CHUNK, 2 * CHUNK))

    def shifted(x, off, mu, keep_carry):
        n = x.shape[1]
        if keep_carry is False:
            carry = jnp.zeros((1, n), F32)
        else:
            carry = prev_ref[0:1, off:off + n]
            if keep_carry is not None:
                carry = jnp.where(keep_carry, carry, 0.0)
        prev_ref[0:1, off:off + n] = x[CHUNK - 1:CHUNK, :]
        return x + (_shift_rows(x, carry) - x) * mu

    def prep_ops(x_ref, s_ref, rows, slot, keep_carry):
        small = {}

        def lora_inputs():
            sm = shifted(s_ref[rows, :].astype(F32), 3 * width, mu_s, keep_carry)
            s1 = sm[:, 0:LANES]
            s2 = sm[:, LANES:LANES + g2_ref.shape[0]]
            small["tanh_w"] = jnp.tanh(s1).astype(BF16)
            small["lin_a"] = s1.astype(BF16)
            small["sig_g"] = _sigmoid(s2).astype(BF16)

        val = {}

        def shift(q):
            lo = q * SLAB
            c = cols[q]
            val[q, "r"] = shifted(x_ref[rows, lo:lo + SLAB].astype(F32), lo, mu_r[:, c], keep_carry)
            val[q, "k"] = shifted(x_ref[rows, width + lo:width + lo + SLAB].astype(F32), width + lo,
                                  mu_k[:, c], keep_carry)
            v = shifted(x_ref[rows, 2 * width + lo:2 * width + lo + SLAB].astype(F32), 2 * width + lo,
                        mu_v[:, c], keep_carry)
            vv_ref[slot, :, c] = v.astype(BF16)
            val[q, "v"] = v

        def lora(q):
            c = cols[q]
            w_log = -_softplus(-(w0[:, c] + _dot(small["tanh_w"], w2_ref[:, c]))) - 0.5
            val[q, "a"] = _sigmoid(a0[:, c] + _dot(small["lin_a"], a2_ref[:, c]))
            gg_ref[slot, :, c] = _dot(small["sig_g"], g2_ref[:, c])
            val[q, "ld"] = -jnp.exp(w_log)

        def keys(q):
            c = cols[q]
            k = val[q, "k"]
            kk = k * k_k[:, c]
            val[q, "kk"] = kk * lax.rsqrt(jnp.maximum(_head_sum(kk * kk, ones_bd), 1e-24))
            val[q, "k"] = k * (1.0 + (val[q, "a"] - 1.0) * k_a[:, c])

        def bonus(q):
            c = cols[q]
            bon_ref[slot, :, c] = _head_sum(val[q, "r"] * val[q, "k"] * r_k[:, c], ones_bd) * val.pop((q, "v"))

        def decay(q):
            cum = val[q, "ld"]
            sh = 1
            while sh < CHUNK:
                cum = cum + jnp.where(row_c >= sh, pltpu.roll(cum, sh, 0), 0.0)
                sh *= 2
            val[q, "cum"] = cum

        def scale_in(q):
            c = cols[q]
            cum, ld, kk, r = val[q, "cum"], val.pop((q, "ld")), val[q, "kk"], val.pop((q, "r"))
            ab_ref[slot, :, c] = (-kk * jnp.exp(cum - ld)).astype(BF16)
            rb_ref[slot, :, c] = (r * jnp.exp(cum)).astype(BF16)

        def scale_out(q):
            c = cols[q]
            cum, kk, a, k = val.pop((q, "cum")), val.pop((q, "kk")), val.pop((q, "a")), val.pop((q, "k"))
            tot = cum[CHUNK - 1:CHUNK, :]
            kb = kk * a
            e_inv = jnp.exp(-cum)
            e_end = jnp.exp(tot - cum)
            bt_ref[slot, :, c] = (kb * e_inv).astype(BF16)
            kt_ref[slot, :, c] = (k * e_inv).astype(BF16)
            bh_ref[slot, :, c] = (kb * e_end).astype(BF16)
            kh_ref[slot, :, c] = (k * e_end).astype(BF16)
            pc_ref[slot, 0:1, c] = jnp.exp(tot)

        stages = (shift, lora, keys, bonus, decay, scale_in, scale_out)
        return [lora_inputs] + [functools.partial(stage, q) for q in slabs for stage in stages]

    def stack(x):
        zero = jnp.zeros_like(x)
        return jnp.concatenate([jnp.where(head_of_lane == hd, x, zero) for hd in range(heads_per_slab)],
                               axis=0)

    def chain_ops(slot, half):
        live = {}

        def gram(q):
            xx = jnp.concatenate([stack(ab_ref[slot, :, cols[q]]), stack(rb_ref[slot, :, cols[q]])], axis=0)
            gb = _dot_nt(xx, stack(bt_ref[slot, :, cols[q]]))
            gk = _dot_nt(xx, stack(kt_ref[slot, :, cols[q]]))
            a_ab = jnp.where(strict, gb[:SLAB], 0.0)
            pw_ref[half, 0, q] = a_ab.astype(BF16)
            tf_ref[half, q] = eye + a_ab
            aak_ref[half, q] = jnp.where(strict, gk[:SLAB], 0.0).astype(BF16)
            arb_ref[half, q] = jnp.where(incl, gb[SLAB:], 0.0).astype(BF16)
            ark_ref[half, q] = jnp.where(incl, gk[SLAB:], 0.0).astype(BF16)

        def square(s, q):
            pw_ref[half, (s + 1) % 2, q] = _dot(pw_ref[half, s % 2, q], pw_ref[half, s % 2, q]).astype(BF16)

        def extend(s, q):
            tf_ref[half, q] = tf_ref[half, q] + _dot(tf_ref[half, q].astype(BF16), pw_ref[half, (s + 1) % 2, q])

        def project(q):
            vs = stack(vv_ref[slot, :, cols[q]])
            xx = jnp.concatenate([stack(ab_ref[slot, :, cols[q]]), stack(rb_ref[slot, :, cols[q]])], axis=0)
            live[q] = (vs, _dot_nt(xx, hs_ref[q].astype(BF16)))

        def rhs(q):
            vs, xh = live[q]
            live[q] = (vs, xh[SLAB:], (xh[:SLAB] + _dot(aak_ref[half, q], vs)).astype(BF16))

        def solve(q):
            vs, xrh, b = live[q]
            live[q] = (vs, xrh, _dot(tf_ref[half, q].astype(BF16), b).astype(BF16))

        def emit(q):
            vs, xrh, u = live[q]
            y_st = xrh + _dot(arb_ref[half, q], u) + _dot(ark_ref[half, q], vs)
            y = y_st[0:CHUNK]
            for hd in range(1, heads_per_slab):
                y = y + y_st[hd * CHUNK:(hd + 1) * CHUNK]
            y_ref[half, :, cols[q]] = y

        def advance(q):
            vs, _, u = live.pop(q)
            pc = pc_ref[slot, 0:1, cols[q]]
            hs_ref[q] = (hs_ref[q] * pc + _dot_tn(u, stack(bh_ref[slot, :, cols[q]]))
                         + _dot_tn(vs, stack(kh_ref[slot, :, cols[q]])))

        stages = [gram]
        for s in range(n_sq):
            stages += [functools.partial(square, s), functools.partial(extend, s)]
        stages += [project, rhs, solve, emit, advance]
        return [functools.partial(stage, q) for stage in stages for q in slabs]

    def post_ops(slot, half):
        val = {}
        inv_n = 1.0 / HEAD_DIM

        def center(q):
            y = y_ref[half, :, cols[q]]
            val[q] = y - _head_sum(y, ones_bd) * inv_n

        def scale(q):
            c = cols[q]
            yc = val.pop(q)
            var = _head_sum(yc * yc, ones_bd) * inv_n
            yn = yc * lax.rsqrt(var + GN_EPS) * ln_w[:, c] + ln_b[:, c]
            o_ref[halves[half], c] = ((yn + bon_ref[slot, :, c]) * gg_ref[slot, :, c]).astype(o_ref.dtype)

        return [functools.partial(stage, q) for q in slabs for stage in (center, scale)]

    def interleave(main, fill):
        n, m = len(main), len(fill)
        done = 0
        for i, op in enumerate(main):
            op()
            while done < m and (done + 1) * n <= (i + 1) * (m + 1):
                fill[done]()
                done += 1
        for op in fill[done:]:
            op()

    def step(chain_slots, prep_slots):
        keep_next = (2 * j + 2) % chunks_per_seq != 0
        interleave(chain_ops(chain_slots[0], 0),
                   prep_ops(xn_ref, sn_ref, halves[0], prep_slots[0], keep_next))
        interleave(chain_ops(chain_slots[1], 1),
                   post_ops(chain_slots[0], 0) + prep_ops(xn_ref, sn_ref, halves[1], prep_slots[1], None))
        for op in post_ops(chain_slots[1], 1):
            op()

    @pl.when(j == 0)
    def _():
        for op in prep_ops(x0_ref, s0_ref, halves[0], 0, False) + prep_ops(x0_ref, s0_ref, halves[1], 1, None):
            op()

    @pl.when((2 * j) % chunks_per_seq == 0)
    def _():
        hs_ref[...] = jnp.zeros_like(hs_ref)

    @pl.when(j % 2 == 0)
    def _():
        step((0, 1), (2, 3))

    @pl.when(j % 2 == 1)
    def _():
        step((2, 3), (0, 1))


def _rwkv(p, vec, mus, w2p, a2p, g2p, *, width, small_col, small_w, chunks_per_seq):
    t = p.shape[0]
    n_chunks = t // CHUNK
    assert chunks_per_seq % 2 == 0 and n_chunks % chunks_per_seq == 0
    n_slabs = width // SLAB
    n_steps = n_chunks // 2
    views = []
    for idx in (lambda j: 0, lambda j: jnp.minimum(j + 1, n_steps - 1)):
        views.append(pl.BlockSpec((2 * CHUNK, 3 * width), lambda j, idx=idx: (idx(j), 0)))
        views.append(pl.BlockSpec((2 * CHUNK, small_w), lambda j, idx=idx: (idx(j), small_col)))
    full = lambda a: pl.BlockSpec(a.shape, lambda j: (0,) * a.ndim)
    n_slots = 4
    slot_bf16 = pltpu.VMEM((n_slots, CHUNK, width), BF16)
    slot_f32 = pltpu.VMEM((n_slots, CHUNK, width), F32)
    return pl.pallas_call(
        functools.partial(_rwkv_kernel, chunks_per_seq=chunks_per_seq),
        grid=(n_steps,),
        in_specs=views + [full(vec), full(mus), full(w2p), full(a2p), full(g2p)],
        out_specs=pl.BlockSpec((2 * CHUNK, width), lambda j: (j, 0)),
        out_shape=jax.ShapeDtypeStruct((t, width), BF16),
        scratch_shapes=[
            pltpu.VMEM((8, 3 * width + small_w), F32),
            pltpu.VMEM((n_slabs, SLAB, SLAB), F32),
            slot_bf16, slot_bf16, slot_bf16, slot_bf16, slot_bf16, slot_bf16, slot_bf16,
            pltpu.VMEM((n_slots, 8, width), F32),
            slot_f32, slot_f32,
            pltpu.VMEM((2, CHUNK, width), F32),
            pltpu.VMEM((2, 2, n_slabs, SLAB, SLAB), BF16),
            pltpu.VMEM((2, n_slabs, SLAB, SLAB), F32),
            pltpu.VMEM((2, n_slabs, SLAB, SLAB), BF16), pltpu.VMEM((2, n_slabs, SLAB, SLAB), BF16),
            pltpu.VMEM((2, n_slabs, SLAB, SLAB), BF16),
        ],
        compiler_params=pltpu.CompilerParams(
            dimension_semantics=("arbitrary",), vmem_limit_bytes=VMEM_LIMIT),
        name="rwkv",
    )(p, p, p, p, vec, mus, w2p, a2p, g2p)


def _lru_kernel(x_ref, gate_ref, vec_ref, w_ref, o_ref, ext_ref, a_ref, u_ref, h_ref, hc_ref):
    ts, width = x_ref.shape
    pad = 8

    @pl.when(pl.program_id(1) == 0)
    def _():
        ext_ref[0:pad, :] = jnp.zeros((pad, width), F32)
        hc_ref[...] = jnp.zeros_like(hc_ref)

    vec = vec_ref[...]
    conv_b, b_a, b_x, lam, norm_g = (vec[i:i + 1, :] for i in range(CONV_WIDTH, CONV_WIDTH + 5))

    x = x_ref[...].astype(F32)
    ext_ref[pad:pad + ts, :] = x
    xc = conv_b + vec[CONV_WIDTH - 1:CONV_WIDTH, :] * x
    for j in range(CONV_WIDTH - 1):
        xc = xc + vec[j:j + 1, :] * ext_ref[pl.ds(pad - (CONV_WIDTH - 1) + j, ts), :]
    ext_ref[0:pad, :] = ext_ref[ts:ts + pad, :]

    xcb = xc.astype(BF16)
    ra, ix = [], []
    for s in range(width // LANES):
        z = _dot(xcb[:, s * LANES:(s + 1) * LANES], w_ref[s])
        ra.append(z[:, :LANES])
        ix.append(z[:, LANES:])
    r = _sigmoid(jnp.concatenate(ra, axis=1) + b_a)
    i = _sigmoid(jnp.concatenate(ix, axis=1) + b_x)
    log_a = (-LRU_C) * r * _softplus(-lam)
    a = jnp.exp(log_a)
    u = jnp.sqrt(-jnp.tanh(log_a) * (a * a + 1.0)) * (i * xc)
    a_ref[...] = a
    u_ref[...] = u

    row = lax.broadcasted_iota(jnp.int32, (8, width), 0)

    def group(gi, hc):
        rows = pl.ds(pl.multiple_of(gi * 8, 8), 8)
        aa = a_ref[rows, :]
        uu = u_ref[rows, :]
        for sh in (1, 2, 4):
            ok = row >= sh
            a_sh = pltpu.roll(aa, sh, 0)
            u_sh = pltpu.roll(uu, sh, 0)
            uu = jnp.where(ok, aa * u_sh + uu, uu)
            aa = jnp.where(ok, aa * a_sh, aa)
        hh = aa * hc + uu
        h_ref[rows, :] = hh
        return jnp.broadcast_to(hh[7:8, :], (8, width))

    hc_ref[...] = lax.fori_loop(0, ts // 8, group, hc_ref[...])

    gate = gate_ref[...].astype(F32)
    gelu = 0.5 * gate * (1.0 + jnp.tanh(0.7978845608028654 * (gate + 0.044715 * gate * gate * gate)))
    y = h_ref[...] * gelu
    o_ref[...] = _rms_norm(y, norm_g).astype(o_ref.dtype)


def _lru(p, vec, w_bd, *, width, x_col, gate_col, ts=256):
    b, s, _ = p.shape
    full = lambda a: pl.BlockSpec(a.shape, lambda bi, ti: (0,) * a.ndim)
    return pl.pallas_call(
        _lru_kernel,
        grid=(b, s // ts),
        in_specs=[pl.BlockSpec((None, ts, width), lambda bi, ti: (bi, ti, x_col)),
                  pl.BlockSpec((None, ts, width), lambda bi, ti: (bi, ti, gate_col)),
                  full(vec), full(w_bd)],
        out_specs=pl.BlockSpec((None, ts, width), lambda bi, ti: (bi, ti, 0)),
        out_shape=jax.ShapeDtypeStruct((b, s, width), BF16),
        scratch_shapes=[
            pltpu.VMEM((ts + 8, width), F32),
            pltpu.VMEM((ts, width), F32), pltpu.VMEM((ts, width), F32), pltpu.VMEM((ts, width), F32),
            pltpu.VMEM((8, width), F32),
        ],
        compiler_params=pltpu.CompilerParams(
            dimension_semantics=("arbitrary", "arbitrary"), vmem_limit_bytes=VMEM_LIMIT),
        name="lru",
    )(p, p, vec, w_bd)


def _pad_rows(w, rows, at):
    out = jnp.zeros((rows, w.shape[1]), w.dtype)
    return out.at[at:at + w.shape[0]].set(w)


def _block_diag_pairs(wa, wx):
    def bd(w):
        h, n, _ = w.shape
        w = w.reshape(h // PAIR, PAIR, n, n)
        z = jnp.zeros_like(w[:, 0])
        top = jnp.concatenate([w[:, 0], z], axis=2)
        bot = jnp.concatenate([z, w[:, 1]], axis=2)
        return jnp.concatenate([top, bot], axis=1)
    return jnp.concatenate([bd(wa), bd(wx)], axis=2)


def kernel(x, ffn1_norm, ffn1_w_gate, ffn1_w_up, ffn1_w_down, mix_norm, w_in, rwkv_mu, rwkv_w0, rwkv_w2, rwkv_a0, rwkv_a2, rwkv_g2, rwkv_k_k, rwkv_k_a, rwkv_r_k, rwkv_ln_w, rwkv_ln_b, lru_conv_w, lru_conv_b, lru_wa, lru_ba, lru_wx, lru_bx, lru_lam, lru_norm, w_out, ffn2_norm, ffn2_w_gate, ffn2_w_up, ffn2_w_down, final_norm):
    bsz, seq, d = x.shape
    depth = w_in.shape[0]
    wr = rwkv_w0.shape[1]
    wl = lru_lam.shape[1]
    n_w, n_a, n_g = rwkv_w2.shape[1], rwkv_a2.shape[1], rwkv_g2.shape[1]
    assert wr == wl and wr % SLAB == 0 and n_w + n_a == LANES and lru_wa.shape[2] == HEAD_DIM
    small_w = 4 * LANES
    g_pad = 2 * LANES
    assert n_g <= g_pad and (3 * wr) % small_w == 0 and seq % (2 * CHUNK) == 0

    row = lambda v: v.reshape(1, -1).astype(F32)
    h = x.reshape(bsz * seq, d)
    for l in range(depth):
        h = _ffn(h, row(ffn1_norm[l]), ffn1_w_gate[l].astype(BF16), ffn1_w_up[l].astype(BF16),
                 ffn1_w_down[l].astype(BF16), row(final_norm), final_norm=False)

        o_r, o_w = 0, wr
        o_k = o_w + n_w
        o_v = o_k + wr
        o_a = o_v + wr
        o_g = o_a + n_a
        o_l = o_g + n_g
        w = w_in[l]
        zpad = jnp.zeros((d, small_w - n_w - n_a - n_g), w.dtype)
        w_re = jnp.concatenate([w[:, o_r:o_r + wr], w[:, o_k:o_k + wr], w[:, o_v:o_v + wr],
                                w[:, o_l:o_l + 2 * wl],
                                w[:, o_w:o_w + n_w], w[:, o_a:o_a + n_a], w[:, o_g:o_g + n_g], zpad],
                               axis=1).astype(BF16)
        mu = rwkv_mu[l]
        mus = jnp.concatenate([mu[o_w:o_w + n_w], mu[o_a:o_a + n_a], mu[o_g:o_g + n_g],
                               jnp.zeros((small_w - n_w - n_a - n_g,), F32)]).reshape(1, -1)
        vec = jnp.stack([rwkv_w0[l], rwkv_a0[l], rwkv_k_k[l], rwkv_k_a[l], rwkv_r_k[l].reshape(-1),
                         rwkv_ln_w[l], rwkv_ln_b[l],
                         mu[o_r:o_r + wr], mu[o_k:o_k + wr], mu[o_v:o_v + wr]]
                        + [jnp.zeros((wr,), F32)] * 6).astype(F32)
        w2p = _pad_rows(rwkv_w2[l], LANES, 0).astype(BF16)
        a2p = _pad_rows(rwkv_a2[l], LANES, n_w).astype(BF16)
        g2p = _pad_rows(rwkv_g2[l], g_pad, 0).astype(BF16)

        lvec = jnp.concatenate([lru_conv_w[l], jnp.stack([lru_conv_b[l], lru_ba[l], lru_bx[l],
                                                          lru_lam[l], lru_norm[l]]),
                                jnp.zeros((16 - CONV_WIDTH - 5, wl), F32)], axis=0).astype(F32)
        w_bd = _block_diag_pairs(lru_wa[l], lru_wx[l]).astype(BF16)

        p = _proj(h, row(mix_norm[l]), w_re)
        y_r = _rwkv(p, vec, mus, w2p, a2p, g2p, width=wr, small_col=(3 * wr + 2 * wl) // small_w,
                    small_w=small_w, chunks_per_seq=seq // CHUNK)
        y_l = _lru(p.reshape(bsz, seq, -1), lvec, w_bd, width=wl, x_col=3, gate_col=4)
        h = _outproj(h, y_r, y_l.reshape(bsz * seq, wl), w_out[l].astype(BF16))
        h = _ffn(h, row(ffn2_norm[l]), ffn2_w_gate[l].astype(BF16), ffn2_w_up[l].astype(BF16),
                 ffn2_w_down[l].astype(BF16), row(final_norm), final_norm=(l == depth - 1))
    return h.reshape(bsz, seq, d)
```

```python
import functools

import jax
import jax.numpy as jnp
from jax import lax
from jax.experimental import pallas as pl
from jax.experimental.pallas import tpu as pltpu

F32 = jnp.float32
BF16 = jnp.bfloat16

HEAD_DIM = 64
CONV_WIDTH = 4
LRU_C = 8.0
NORM_EPS = 1e-6
GN_EPS = 64e-5

LANES = 128
CHUNK = 64
PAIR = LANES // HEAD_DIM
SLAB = 256

VMEM_PHYSICAL_V7X = 64 * 1024 * 1024
VMEM_LIMIT = VMEM_PHYSICAL_V7X - 4 * 1024 * 1024


def _dot(a, b):
    return jnp.dot(a, b, preferred_element_type=F32)


def _dot_nt(a, b):
    return lax.dot_general(a, b, (((1,), (1,)), ((), ())), preferred_element_type=F32)


def _dot_tn(a, b):
    return lax.dot_general(a, b, (((0,), (0,)), ((), ())), preferred_element_type=F32)


def _sigmoid(x):
    return 0.5 * (jnp.tanh(0.5 * x) + 1.0)


def _softplus(x):
    return jnp.maximum(x, 0.0) + jnp.log(1.0 + jnp.exp(-jnp.abs(x)))


def _rms_norm(x, g):
    ms = jnp.mean(x * x, axis=-1, keepdims=True)
    return x * lax.rsqrt(ms + NORM_EPS) * g


def _ffn_kernel(h_ref, g_ref, wg_ref, wu_ref, wd_ref, fg_ref, o_ref, xn_ref, *, final_norm):
    j = pl.program_id(1)

    @pl.when(j == 0)
    def _():
        xn_ref[...] = _rms_norm(h_ref[...], g_ref[...]).astype(BF16)
        o_ref[...] = jnp.zeros_like(o_ref)

    xn = xn_ref[...]
    gate = _dot(xn, wg_ref[...])
    up = _dot(xn, wu_ref[...])
    act = (gate * _sigmoid(gate) * up).astype(BF16)
    o_ref[...] += _dot(act, wd_ref[...])

    @pl.when(j == pl.num_programs(1) - 1)
    def _():
        out = h_ref[...] + 0.5 * o_ref[...]
        if final_norm:
            out = _rms_norm(out, fg_ref[...])
        o_ref[...] = out


def _ffn(h, g, wg, wu, wd, fg, *, final_norm, tf=512):
    t, d = h.shape
    tm = 512 if final_norm else 1024
    dff = wg.shape[1]
    grid = (t // tm, dff // tf)
    return pl.pallas_call(
        functools.partial(_ffn_kernel, final_norm=final_norm),
        grid=grid,
        in_specs=[
            pl.BlockSpec((tm, d), lambda i, j: (i, 0)),
            pl.BlockSpec((1, d), lambda i, j: (0, 0)),
            pl.BlockSpec((d, tf), lambda i, j: (0, j)),
            pl.BlockSpec((d, tf), lambda i, j: (0, j)),
            pl.BlockSpec((tf, d), lambda i, j: (j, 0)),
            pl.BlockSpec((1, d), lambda i, j: (0, 0)),
        ],
        out_specs=pl.BlockSpec((tm, d), lambda i, j: (i, 0)),
        out_shape=jax.ShapeDtypeStruct((t, d), F32),
        scratch_shapes=[pltpu.VMEM((tm, d), BF16)],
        compiler_params=pltpu.CompilerParams(
            dimension_semantics=("parallel", "arbitrary"), vmem_limit_bytes=VMEM_LIMIT),
        name="ffn",
    )(h, g, wg, wu, wd, fg)


def _proj_kernel(h_ref, g_ref, w_ref, o_ref, xn_ref):
    @pl.when(pl.program_id(1) == 0)
    def _():
        xn_ref[...] = _rms_norm(h_ref[...], g_ref[...]).astype(BF16)

    o_ref[...] = _dot(xn_ref[...], w_ref[...]).astype(o_ref.dtype)


def _proj(h, g, w, *, tm=1024, tn=512):
    t, d = h.shape
    n = w.shape[1]
    return pl.pallas_call(
        _proj_kernel,
        grid=(t // tm, n // tn),
        in_specs=[
            pl.BlockSpec((tm, d), lambda i, j: (i, 0)),
            pl.BlockSpec((1, d), lambda i, j: (0, 0)),
            pl.BlockSpec((d, tn), lambda i, j: (0, j)),
        ],
        out_specs=pl.BlockSpec((tm, tn), lambda i, j: (i, j)),
        out_shape=jax.ShapeDtypeStruct((t, n), BF16),
        scratch_shapes=[pltpu.VMEM((tm, d), BF16)],
        compiler_params=pltpu.CompilerParams(
            dimension_semantics=("parallel", "arbitrary"), vmem_limit_bytes=VMEM_LIMIT),
        name="proj",
    )(h, g, w)


def _outproj_kernel(h_ref, ya_ref, yb_ref, w_ref, o_ref):
    ka = ya_ref.shape[1]
    o_ref[...] = (h_ref[...] + _dot(ya_ref[...], w_ref[:ka, :]) + _dot(yb_ref[...], w_ref[ka:, :]))


def _outproj(h, ya, yb, w, *, tm=512):
    t, d = h.shape
    ka, kb = ya.shape[1], yb.shape[1]
    return pl.pallas_call(
        _outproj_kernel,
        grid=(t // tm,),
        in_specs=[
            pl.BlockSpec((tm, d), lambda i: (i, 0)),
            pl.BlockSpec((tm, ka), lambda i: (i, 0)),
            pl.BlockSpec((tm, kb), lambda i: (i, 0)),
            pl.BlockSpec((ka + kb, d), lambda i: (0, 0)),
        ],
        out_specs=pl.BlockSpec((tm, d), lambda i: (i, 0)),
        out_shape=jax.ShapeDtypeStruct((t, d), F32),
        compiler_params=pltpu.CompilerParams(
            dimension_semantics=("parallel",), vmem_limit_bytes=VMEM_LIMIT),
        name="outproj",
    )(h, ya, yb, w)


def _head_sum(x, ones_bd, exact):
    hi = x.astype(BF16)
    if not exact:
        return _dot(hi, ones_bd)
    r = x.shape[0]
    lo = (x - hi.astype(F32)).astype(BF16)
    z = _dot(jnp.concatenate([hi, lo], axis=0), ones_bd)
    return z[:r] + z[r:]


def _shift_rows(x, carry_row):
    rolled = pltpu.roll(x, 1, 0)
    row = lax.broadcasted_iota(jnp.int32, x.shape, 0)
    return jnp.where(row == 0, carry_row, rolled)


def _rwkv_kernel(x0_ref, s0_ref, xn_ref, sn_ref,
                 vec_ref, mus_ref, w2_ref, a2_ref, g2_ref,
                 o_ref,
                 prev_ref, hs_ref, ab_ref, bt_ref, kt_ref, rb_ref, vv_ref, bh_ref, kh_ref,
                 pc_ref, bon_ref, gg_ref, y_ref, pw_ref, tf_ref, aak_ref, arb_ref, ark_ref,
                 *, chunks_per_seq, n_steps):
    j = pl.program_id(0)
    width = o_ref.shape[1]
    n_slabs = width // SLAB
    heads_per_slab = SLAB // HEAD_DIM

    vec = vec_ref[...]
    w0, a0, k_k, k_a, r_k, ln_w, ln_b, mu_r, mu_k, mu_v = (vec[i:i + 1, :] for i in range(10))
    mu_s = mus_ref[...]

    lane_s = lax.broadcasted_iota(jnp.int32, (SLAB, SLAB), 1)
    sub_s = lax.broadcasted_iota(jnp.int32, (SLAB, SLAB), 0)
    strict = lane_s < sub_s
    incl = lane_s <= sub_s
    eye = jnp.where(lane_s == sub_s, 1.0, 0.0).astype(F32)
    ones_bd = jnp.where((lane_s // HEAD_DIM) == (sub_s // HEAD_DIM), 1.0, 0.0).astype(BF16)
    head_of_lane = lax.broadcasted_iota(jnp.int32, (CHUNK, SLAB), 1) // HEAD_DIM
    row_c = lax.broadcasted_iota(jnp.int32, (CHUNK, SLAB), 0)
    cols = [slice(q * SLAB, (q + 1) * SLAB) for q in range(n_slabs)]
    slabs = range(n_slabs)
    n_sq = CHUNK.bit_length() - 2
    halves = (slice(0, CHUNK), slice(CHUNK, 2 * CHUNK))

    def shifted(x, off, mu, keep_carry):
        n = x.shape[1]
        if keep_carry is False:
            carry = jnp.zeros((1, n), F32)
        else:
            carry = prev_ref[0:1, off:off + n]
            if keep_carry is not None:
                carry = jnp.where(keep_carry, carry, 0.0)
        prev_ref[0:1, off:off + n] = x[CHUNK - 1:CHUNK, :]
        return x + (_shift_rows(x, carry) - x) * mu

    def prep_ops(x_ref, s_ref, rows, slot, keep_carry):
        small = {}

        def lora_inputs():
            sm = shifted(s_ref[rows, :].astype(F32), 3 * width, mu_s, keep_carry)
            s1 = sm[:, 0:LANES]
            s2 = sm[:, LANES:LANES + g2_ref.shape[0]]
            small["tanh_w"] = jnp.tanh(s1).astype(BF16)
            small["lin_a"] = s1.astype(BF16)
            small["sig_g"] = _sigmoid(s2).astype(BF16)

        val = {}

        def shift(q):
            lo = q * SLAB
            c = cols[q]
            val[q, "r"] = shifted(x_ref[rows, lo:lo + SLAB].astype(F32), lo, mu_r[:, c], keep_carry)
            val[q, "k"] = shifted(x_ref[rows, width + lo:width + lo + SLAB].astype(F32), width + lo,
                                  mu_k[:, c], keep_carry)
            v = shifted(x_ref[rows, 2 * width + lo:2 * width + lo + SLAB].astype(F32), 2 * width + lo,
                        mu_v[:, c], keep_carry)
            vv_ref[slot, :, c] = v.astype(BF16)
            val[q, "v"] = v

        def lora(q):
            c = cols[q]
            w_log = -_softplus(-(w0[:, c] + _dot(small["tanh_w"], w2_ref[:, c]))) - 0.5
            val[q, "a"] = _sigmoid(a0[:, c] + _dot(small["lin_a"], a2_ref[:, c]))
            gg_ref[slot, :, c] = _dot(small["sig_g"], g2_ref[:, c])
            val[q, "ld"] = -jnp.exp(w_log)

        def keys(q):
            c = cols[q]
            k = val[q, "k"]
            kk = k * k_k[:, c]
            val[q, "kk"] = kk * lax.rsqrt(jnp.maximum(_head_sum(kk * kk, ones_bd, exact=True), 1e-24))
            val[q, "k"] = k * (1.0 + (val[q, "a"] - 1.0) * k_a[:, c])

        def bonus(q):
            c = cols[q]
            rk = val[q, "r"] * val[q, "k"] * r_k[:, c]
            bon_ref[slot, :, c] = _head_sum(rk, ones_bd, exact=False) * val.pop((q, "v"))

        def decay(q):
            cum = val[q, "ld"]
            sh = 1
            while sh < CHUNK:
                cum = cum + jnp.where(row_c >= sh, pltpu.roll(cum, sh, 0), 0.0)
                sh *= 2
            val[q, "cum"] = cum

        def scale_in(q):
            c = cols[q]
            cum, ld, kk, r = val[q, "cum"], val.pop((q, "ld")), val[q, "kk"], val.pop((q, "r"))
            ab_ref[slot, :, c] = (-kk * jnp.exp(cum - ld)).astype(BF16)
            rb_ref[slot, :, c] = (r * jnp.exp(cum)).astype(BF16)

        def scale_out(q):
            c = cols[q]
            cum, kk, a, k = val.pop((q, "cum")), val.pop((q, "kk")), val.pop((q, "a")), val.pop((q, "k"))
            tot = cum[CHUNK - 1:CHUNK, :]
            kb = kk * a
            e_inv = jnp.exp(-cum)
            e_end = jnp.exp(tot - cum)
            bt_ref[slot, :, c] = (kb * e_inv).astype(BF16)
            kt_ref[slot, :, c] = (k * e_inv).astype(BF16)
            bh_ref[slot, :, c] = (kb * e_end).astype(BF16)
            kh_ref[slot, :, c] = (k * e_end).astype(BF16)
            pc_ref[slot, 0:1, c] = jnp.exp(tot)

        stages = (shift, lora, keys, bonus, decay, scale_in, scale_out)
        return [lora_inputs] + [functools.partial(stage, q) for q in slabs for stage in stages]

    def stack(x):
        zero = jnp.zeros_like(x)
        return jnp.concatenate([jnp.where(head_of_lane == hd, x, zero) for hd in range(heads_per_slab)],
                               axis=0)

    def chain_ops(slot, half, ybuf):
        live = {}

        def gram(q):
            xx = jnp.concatenate([stack(ab_ref[slot, :, cols[q]]), stack(rb_ref[slot, :, cols[q]])], axis=0)
            gb = _dot_nt(xx, stack(bt_ref[slot, :, cols[q]]))
            gk = _dot_nt(xx, stack(kt_ref[slot, :, cols[q]]))
            a_ab = jnp.where(strict, gb[:SLAB], 0.0)
            pw_ref[half, 0, q] = a_ab.astype(BF16)
            tf_ref[half, q] = eye + a_ab
            aak_ref[half, q] = jnp.where(strict, gk[:SLAB], 0.0).astype(BF16)
            arb_ref[half, q] = jnp.where(incl, gb[SLAB:], 0.0).astype(BF16)
            ark_ref[half, q] = jnp.where(incl, gk[SLAB:], 0.0).astype(BF16)

        def square(s, q):
            pw_ref[half, (s + 1) % 2, q] = _dot(pw_ref[half, s % 2, q], pw_ref[half, s % 2, q]).astype(BF16)

        def extend(s, q):
            tf_ref[half, q] = tf_ref[half, q] + _dot(tf_ref[half, q].astype(BF16), pw_ref[half, (s + 1) % 2, q])

        def project(q):
            vs = stack(vv_ref[slot, :, cols[q]])
            xx = jnp.concatenate([stack(ab_ref[slot, :, cols[q]]), stack(rb_ref[slot, :, cols[q]])], axis=0)
            live[q] = (vs, _dot_nt(xx, hs_ref[q].astype(BF16)))

        def rhs(q):
            vs, xh = live[q]
            live[q] = (vs, xh[SLAB:], (xh[:SLAB] + _dot(aak_ref[half, q], vs)).astype(BF16))

        def solve(q):
            vs, xrh, b = live[q]
            live[q] = (vs, xrh, _dot(tf_ref[half, q].astype(BF16), b).astype(BF16))

        def emit(q):
            vs, xrh, u = live[q]
            y_st = xrh + _dot(arb_ref[half, q], u) + _dot(ark_ref[half, q], vs)
            y = y_st[0:CHUNK]
            for hd in range(1, heads_per_slab):
                y = y + y_st[hd * CHUNK:(hd + 1) * CHUNK]
            y_ref[ybuf, half, :, cols[q]] = y

        def advance(q):
            vs, _, u = live.pop(q)
            pc = pc_ref[slot, 0:1, cols[q]]
            hs_ref[q] = (hs_ref[q] * pc + _dot_tn(u, stack(bh_ref[slot, :, cols[q]]))
                         + _dot_tn(vs, stack(kh_ref[slot, :, cols[q]])))

        free = [gram]
        for s in range(n_sq):
            free += [functools.partial(square, s), functools.partial(extend, s)]
        bound = [project, rhs, solve, emit, advance]
        per_slab = lambda stages: [[functools.partial(stage, q) for q in slabs] for stage in stages]
        return per_slab(free), per_slab(bound)

    def post_ops(slot, half, ybuf):
        val = {}
        inv_n = 1.0 / HEAD_DIM

        def center(q):
            y = y_ref[ybuf, half, :, cols[q]]
            val[q] = y - _head_sum(y, ones_bd, exact=False) * inv_n

        def scale(q):
            c = cols[q]
            yc = val.pop(q)
            var = _head_sum(yc * yc, ones_bd, exact=False) * inv_n
            yn = yc * lax.rsqrt(var + GN_EPS) * ln_w[:, c] + ln_b[:, c]
            o_ref[halves[half], c] = ((yn + bon_ref[slot, :, c]) * gg_ref[slot, :, c]).astype(o_ref.dtype)

        return [functools.partial(stage, q) for q in slabs for stage in (center, scale)]

    def interleave(main, fill):
        n, m = len(main), len(fill)
        done = 0
        for i, op in enumerate(main):
            op()
            while done < m and (done + 1) * n <= (i + 1) * (m + 1):
                fill[done]()
                done += 1
        for op in fill[done:]:
            op()

    slot_sets = ((0, 1), (2, 3))

    def step(parity):
        chain_slots, prep_slots = slot_sets[parity], slot_sets[1 - parity]
        keep_next = (2 * j + 2) % chunks_per_seq != 0
        free0, bound0 = chain_ops(chain_slots[0], 0, parity)
        free1, bound1 = chain_ops(chain_slots[1], 1, parity)
        main = [op for pair in zip(free0, free1) for stage in pair for op in stage]
        main += [op for stage in bound0 + bound1 for op in stage]
        fill = []
        for half in range(2):
            fill += post_ops(prep_slots[half], half, 1 - parity)
            fill += prep_ops(xn_ref, sn_ref, halves[half], prep_slots[half], keep_next if half == 0 else None)
        interleave(main, fill)

    @pl.when(j == 0)
    def _():
        y_ref[1] = jnp.zeros_like(y_ref[1])
        for slot in slot_sets[1]:
            bon_ref[slot] = jnp.zeros_like(bon_ref[slot])
            gg_ref[slot] = jnp.zeros_like(gg_ref[slot])
        for op in prep_ops(x0_ref, s0_ref, halves[0], 0, False) + prep_ops(x0_ref, s0_ref, halves[1], 1, None):
            op()

    @pl.when((2 * j) % chunks_per_seq == 0)
    def _():
        hs_ref[...] = jnp.zeros_like(hs_ref)

    for parity in range(2):
        @pl.when((j % 2 == parity) & (j < n_steps))
        def _():
            step(parity)

    @pl.when(j == n_steps)
    def _():
        last = n_steps % 2
        for half in range(2):
            for op in post_ops(slot_sets[1 - last][half], half, 1 - last):
                op()


def _rwkv(p, vec, mus, w2p, a2p, g2p, *, width, small_col, small_w, chunks_per_seq):
    t = p.shape[0]
    n_chunks = t // CHUNK
    assert chunks_per_seq % 2 == 0 and n_chunks % chunks_per_seq == 0
    n_slabs = width // SLAB
    n_steps = n_chunks // 2
    views = []
    for idx in (lambda j: 0, lambda j: jnp.minimum(j + 1, n_steps - 1)):
        views.append(pl.BlockSpec((2 * CHUNK, 3 * width), lambda j, idx=idx: (idx(j), 0)))
        views.append(pl.BlockSpec((2 * CHUNK, small_w), lambda j, idx=idx: (idx(j), small_col)))
    full = lambda a: pl.BlockSpec(a.shape, lambda j: (0,) * a.ndim)
    n_slots = 4
    slot_bf16 = pltpu.VMEM((n_slots, CHUNK, width), BF16)
    slot_f32 = pltpu.VMEM((n_slots, CHUNK, width), F32)
    return pl.pallas_call(
        functools.partial(_rwkv_kernel, chunks_per_seq=chunks_per_seq, n_steps=n_steps),
        grid=(n_steps + 1,),
        in_specs=views + [full(vec), full(mus), full(w2p), full(a2p), full(g2p)],
        out_specs=pl.BlockSpec((2 * CHUNK, width), lambda j: (jnp.maximum(j - 1, 0), 0)),
        out_shape=jax.ShapeDtypeStruct((t, width), BF16),
        scratch_shapes=[
            pltpu.VMEM((8, 3 * width + small_w), F32),
            pltpu.VMEM((n_slabs, SLAB, SLAB), F32),
            slot_bf16, slot_bf16, slot_bf16, slot_bf16, slot_bf16, slot_bf16, slot_bf16,
            pltpu.VMEM((n_slots, 8, width), F32),
            slot_f32, slot_f32,
            pltpu.VMEM((2, 2, CHUNK, width), F32),
            pltpu.VMEM((2, 2, n_slabs, SLAB, SLAB), BF16),
            pltpu.VMEM((2, n_slabs, SLAB, SLAB), F32),
            pltpu.VMEM((2, n_slabs, SLAB, SLAB), BF16), pltpu.VMEM((2, n_slabs, SLAB, SLAB), BF16),
            pltpu.VMEM((2, n_slabs, SLAB, SLAB), BF16),
        ],
        compiler_params=pltpu.CompilerParams(
            dimension_semantics=("arbitrary",), vmem_limit_bytes=VMEM_LIMIT),
        name="rwkv",
    )(p, p, p, p, vec, mus, w2p, a2p, g2p)


def _lru_kernel(x_ref, gate_ref, vec_ref, w_ref, o_ref, ext_ref, a_ref, u_ref, h_ref, hc_ref):
    ts, width = x_ref.shape
    pad = 8

    @pl.when(pl.program_id(1) == 0)
    def _():
        ext_ref[0:pad, :] = jnp.zeros((pad, width), F32)
        hc_ref[...] = jnp.zeros_like(hc_ref)

    vec = vec_ref[...]
    conv_b, b_a, b_x, lam, norm_g = (vec[i:i + 1, :] for i in range(CONV_WIDTH, CONV_WIDTH + 5))

    x = x_ref[...].astype(F32)
    ext_ref[pad:pad + ts, :] = x
    xc = conv_b + vec[CONV_WIDTH - 1:CONV_WIDTH, :] * x
    for j in range(CONV_WIDTH - 1):
        xc = xc + vec[j:j + 1, :] * ext_ref[pl.ds(pad - (CONV_WIDTH - 1) + j, ts), :]
    ext_ref[0:pad, :] = ext_ref[ts:ts + pad, :]

    xcb = xc.astype(BF16)
    ra, ix = [], []
    for s in range(width // LANES):
        z = _dot(xcb[:, s * LANES:(s + 1) * LANES], w_ref[s])
        ra.append(z[:, :LANES])
        ix.append(z[:, LANES:])
    r = _sigmoid(jnp.concatenate(ra, axis=1) + b_a)
    i = _sigmoid(jnp.concatenate(ix, axis=1) + b_x)
    log_a = (-LRU_C) * r * _softplus(-lam)
    a = jnp.exp(log_a)
    u = jnp.sqrt(-jnp.tanh(log_a) * (a * a + 1.0)) * (i * xc)
    a_ref[...] = a
    u_ref[...] = u

    row = lax.broadcasted_iota(jnp.int32, (8, width), 0)

    def group(gi, hc):
        rows = pl.ds(pl.multiple_of(gi * 8, 8), 8)
        aa = a_ref[rows, :]
        uu = u_ref[rows, :]
        for sh in (1, 2, 4):
            ok = row >= sh
            a_sh = pltpu.roll(aa, sh, 0)
            u_sh = pltpu.roll(uu, sh, 0)
            uu = jnp.where(ok, aa * u_sh + uu, uu)
            aa = jnp.where(ok, aa * a_sh, aa)
        hh = aa * hc + uu
        h_ref[rows, :] = hh
        return jnp.broadcast_to(hh[7:8, :], (8, width))

    hc_ref[...] = lax.fori_loop(0, ts // 8, group, hc_ref[...])

    gate = gate_ref[...].astype(F32)
    gelu = 0.5 * gate * (1.0 + jnp.tanh(0.7978845608028654 * (gate + 0.044715 * gate * gate * gate)))
    y = h_ref[...] * gelu
    o_ref[...] = _rms_norm(y, norm_g).astype(o_ref.dtype)


def _lru(p, vec, w_bd, *, width, x_col, gate_col, ts=256):
    b, s, _ = p.shape
    full = lambda a: pl.BlockSpec(a.shape, lambda bi, ti: (0,) * a.ndim)
    return pl.pallas_call(
        _lru_kernel,
        grid=(b, s // ts),
        in_specs=[pl.BlockSpec((None, ts, width), lambda bi, ti: (bi, ti, x_col)),
                  pl.BlockSpec((None, ts, width), lambda bi, ti: (bi, ti, gate_col)),
                  full(vec), full(w_bd)],
        out_specs=pl.BlockSpec((None, ts, width), lambda bi, ti: (bi, ti, 0)),
        out_shape=jax.ShapeDtypeStruct((b, s, width), BF16),
        scratch_shapes=[
            pltpu.VMEM((ts + 8, width), F32),
            pltpu.VMEM((ts, width), F32), pltpu.VMEM((ts, width), F32), pltpu.VMEM((ts, width), F32),
            pltpu.VMEM((8, width), F32),
        ],
        compiler_params=pltpu.CompilerParams(
            dimension_semantics=("arbitrary", "arbitrary"), vmem_limit_bytes=VMEM_LIMIT),
        name="lru",
    )(p, p, vec, w_bd)


def _pad_rows(w, rows, at):
    out = jnp.zeros((rows, w.shape[1]), w.dtype)
    return out.at[at:at + w.shape[0]].set(w)


def _block_diag_pairs(wa, wx):
    def bd(w):
        h, n, _ = w.shape
        w = w.reshape(h // PAIR, PAIR, n, n)
        z = jnp.zeros_like(w[:, 0])
        top = jnp.concatenate([w[:, 0], z], axis=2)
        bot = jnp.concatenate([z, w[:, 1]], axis=2)
        return jnp.concatenate([top, bot], axis=1)
    return jnp.concatenate([bd(wa), bd(wx)], axis=2)


def kernel(x, ffn1_norm, ffn1_w_gate, ffn1_w_up, ffn1_w_down, mix_norm, w_in, rwkv_mu, rwkv_w0, rwkv_w2, rwkv_a0, rwkv_a2, rwkv_g2, rwkv_k_k, rwkv_k_a, rwkv_r_k, rwkv_ln_w, rwkv_ln_b, lru_conv_w, lru_conv_b, lru_wa, lru_ba, lru_wx, lru_bx, lru_lam, lru_norm, w_out, ffn2_norm, ffn2_w_gate, ffn2_w_up, ffn2_w_down, final_norm):
    bsz, seq, d = x.shape
    depth = w_in.shape[0]
    wr = rwkv_w0.shape[1]
    wl = lru_lam.shape[1]
    n_w, n_a, n_g = rwkv_w2.shape[1], rwkv_a2.shape[1], rwkv_g2.shape[1]
    assert wr == wl and wr % SLAB == 0 and n_w + n_a == LANES and lru_wa.shape[2] == HEAD_DIM
    small_w = 4 * LANES
    g_pad = 2 * LANES
    assert n_g <= g_pad and (3 * wr) % small_w == 0 and seq % (2 * CHUNK) == 0

    row = lambda v: v.reshape(1, -1).astype(F32)
    h = x.reshape(bsz * seq, d)
    for l in range(depth):
        h = _ffn(h, row(ffn1_norm[l]), ffn1_w_gate[l].astype(BF16), ffn1_w_up[l].astype(BF16),
                 ffn1_w_down[l].astype(BF16), row(final_norm), final_norm=False)

        o_r, o_w = 0, wr
        o_k = o_w + n_w
        o_v = o_k + wr
        o_a = o_v + wr
        o_g = o_a + n_a
        o_l = o_g + n_g
        w = w_in[l]
        zpad = jnp.zeros((d, small_w - n_w - n_a - n_g), w.dtype)
        w_re = jnp.concatenate([w[:, o_r:o_r + wr], w[:, o_k:o_k + wr], w[:, o_v:o_v + wr],
                                w[:, o_l:o_l + 2 * wl],
                                w[:, o_w:o_w + n_w], w[:, o_a:o_a + n_a], w[:, o_g:o_g + n_g], zpad],
                               axis=1).astype(BF16)
        mu = rwkv_mu[l]
        mus = jnp.concatenate([mu[o_w:o_w + n_w], mu[o_a:o_a + n_a], mu[o_g:o_g + n_g],
                               jnp.zeros((small_w - n_w - n_a - n_g,), F32)]).reshape(1, -1)
        vec = jnp.stack([rwkv_w0[l], rwkv_a0[l], rwkv_k_k[l], rwkv_k_a[l], rwkv_r_k[l].reshape(-1),
                         rwkv_ln_w[l], rwkv_ln_b[l],
                         mu[o_r:o_r + wr], mu[o_k:o_k + wr], mu[o_v:o_v + wr]]
                        + [jnp.zeros((wr,), F32)] * 6).astype(F32)
        w2p = _pad_rows(rwkv_w2[l], LANES, 0).astype(BF16)
        a2p = _pad_rows(rwkv_a2[l], LANES, n_w).astype(BF16)
        g2p = _pad_rows(rwkv_g2[l], g_pad, 0).astype(BF16)

        lvec = jnp.concatenate([lru_conv_w[l], jnp.stack([lru_conv_b[l], lru_ba[l], lru_bx[l],
                                                          lru_lam[l], lru_norm[l]]),
                                jnp.zeros((16 - CONV_WIDTH - 5, wl), F32)], axis=0).astype(F32)
        w_bd = _block_diag_pairs(lru_wa[l], lru_wx[l]).astype(BF16)

        p = _proj(h, row(mix_norm[l]), w_re)
        y_r = _rwkv(p, vec, mus, w2p, a2p, g2p, width=wr, small_col=(3 * wr + 2 * wl) // small_w,
                    small_w=small_w, chunks_per_seq=seq // CHUNK)
        y_l = _lru(p.reshape(bsz, seq, -1), lvec, w_bd, width=wl, x_col=3, gate_col=4)
        h = _outproj(h, y_r, y_l.reshape(bsz * seq, wl), w_out[l].astype(BF16))
        h = _ffn(h, row(ffn2_norm[l]), ffn2_w_gate[l].astype(BF16), ffn2_w_up[l].astype(BF16),
                 ffn2_w_down[l].astype(BF16), row(final_norm), final_norm=(l == depth - 1))
    return h.reshape(bsz, seq, d)
```

```python
import functools

import jax
import jax.numpy as jnp
from jax import lax
from jax.experimental import pallas as pl
from jax.experimental.pallas import tpu as pltpu

F32 = jnp.float32
BF16 = jnp.bfloat16

HEAD_DIM = 64
CONV_WIDTH = 4
LRU_C = 8.0
NORM_EPS = 1e-6
GN_EPS = 64e-5

LANES = 128
CHUNK = 64
PAIR = LANES // HEAD_DIM
SLAB = 256
VMEM_PHYSICAL_V7X = 64 * 1024 * 1024
VMEM_LIMIT = VMEM_PHYSICAL_V7X - 4 * 1024 * 1024


def _dot(a, b):
    return jnp.dot(a, b, preferred_element_type=F32)


def _dot_nt(a, b):
    return lax.dot_general(a, b, (((1,), (1,)), ((), ())), preferred_element_type=F32)


def _dot_tn(a, b):
    return lax.dot_general(a, b, (((0,), (0,)), ((), ())), preferred_element_type=F32)


def _sigmoid(x):
    return 0.5 * (jnp.tanh(0.5 * x) + 1.0)


def _softplus(x):
    return jnp.maximum(x, 0.0) + jnp.log(1.0 + jnp.exp(-jnp.abs(x)))


def _rms_norm(x, g):
    ms = jnp.mean(x * x, axis=-1, keepdims=True)
    return x * lax.rsqrt(ms + NORM_EPS) * g


def _ffn_kernel(h_ref, g_ref, wg_ref, wu_ref, wd_ref, fg_ref, o_ref, xn_ref, *, final_norm):
    j = pl.program_id(1)

    @pl.when(j == 0)
    def _():
        xn_ref[...] = _rms_norm(h_ref[...], g_ref[...]).astype(BF16)
        o_ref[...] = jnp.zeros_like(o_ref)

    xn = xn_ref[...]
    gate = _dot(xn, wg_ref[...])
    up = _dot(xn, wu_ref[...])
    act = (gate * _sigmoid(gate) * up).astype(BF16)
    o_ref[...] += _dot(act, wd_ref[...])

    @pl.when(j == pl.num_programs(1) - 1)
    def _():
        out = h_ref[...] + 0.5 * o_ref[...]
        if final_norm:
            out = _rms_norm(out, fg_ref[...])
        o_ref[...] = out


def _ffn(h, g, wg, wu, wd, fg, *, final_norm, tf=512):
    t, d = h.shape
    tm = 512 if final_norm else 1024
    dff = wg.shape[1]
    grid = (t // tm, dff // tf)
    return pl.pallas_call(
        functools.partial(_ffn_kernel, final_norm=final_norm),
        grid=grid,
        in_specs=[
            pl.BlockSpec((tm, d), lambda i, j: (i, 0)),
            pl.BlockSpec((1, d), lambda i, j: (0, 0)),
            pl.BlockSpec((d, tf), lambda i, j: (0, j)),
            pl.BlockSpec((d, tf), lambda i, j: (0, j)),
            pl.BlockSpec((tf, d), lambda i, j: (j, 0)),
            pl.BlockSpec((1, d), lambda i, j: (0, 0)),
        ],
        out_specs=pl.BlockSpec((tm, d), lambda i, j: (i, 0)),
        out_shape=jax.ShapeDtypeStruct((t, d), F32),
        scratch_shapes=[pltpu.VMEM((tm, d), BF16)],
        compiler_params=pltpu.CompilerParams(
            dimension_semantics=("parallel", "arbitrary"), vmem_limit_bytes=VMEM_LIMIT),
        name="ffn",
    )(h, g, wg, wu, wd, fg)


def _proj_kernel(h_ref, g_ref, w_ref, o_ref, xn_ref):
    @pl.when(pl.program_id(1) == 0)
    def _():
        xn_ref[...] = _rms_norm(h_ref[...], g_ref[...]).astype(BF16)

    o_ref[...] = _dot(xn_ref[...], w_ref[...]).astype(o_ref.dtype)


def _proj(h, g, w, *, tm=512, tn=2816):
    t, d = h.shape
    n = w.shape[1]
    return pl.pallas_call(
        _proj_kernel,
        grid=(t // tm, n // tn),
        in_specs=[
            pl.BlockSpec((tm, d), lambda i, j: (i, 0)),
            pl.BlockSpec((1, d), lambda i, j: (0, 0)),
            pl.BlockSpec((d, tn), lambda i, j: (0, j)),
        ],
        out_specs=pl.BlockSpec((tm, tn), lambda i, j: (i, j)),
        out_shape=jax.ShapeDtypeStruct((t, n), BF16),
        scratch_shapes=[pltpu.VMEM((tm, d), BF16)],
        compiler_params=pltpu.CompilerParams(
            dimension_semantics=("parallel", "arbitrary"), vmem_limit_bytes=VMEM_LIMIT),
        name="proj",
    )(h, g, w)


def _outproj_kernel(h_ref, ya_ref, yb_ref, w_ref, o_ref):
    ka = ya_ref.shape[1]
    o_ref[...] = (h_ref[...] + _dot(ya_ref[...], w_ref[:ka, :]) + _dot(yb_ref[...], w_ref[ka:, :]))


def _outproj(h, ya, yb, w, *, tm=512):
    t, d = h.shape
    ka, kb = ya.shape[1], yb.shape[1]
    return pl.pallas_call(
        _outproj_kernel,
        grid=(t // tm,),
        in_specs=[
            pl.BlockSpec((tm, d), lambda i: (i, 0)),
            pl.BlockSpec((tm, ka), lambda i: (i, 0)),
            pl.BlockSpec((tm, kb), lambda i: (i, 0)),
            pl.BlockSpec((ka + kb, d), lambda i: (0, 0)),
        ],
        out_specs=pl.BlockSpec((tm, d), lambda i: (i, 0)),
        out_shape=jax.ShapeDtypeStruct((t, d), F32),
        compiler_params=pltpu.CompilerParams(
            dimension_semantics=("parallel",), vmem_limit_bytes=VMEM_LIMIT),
        name="outproj",
    )(h, ya, yb, w)


def _head_sum(x, ones_bd, exact):
    hi = x.astype(BF16)
    if not exact:
        return _dot(hi, ones_bd)
    r = x.shape[0]
    lo = (x - hi.astype(F32)).astype(BF16)
    z = _dot(jnp.concatenate([hi, lo], axis=0), ones_bd)
    return z[:r] + z[r:]


def _shift_rows(x, carry_row):
    rolled = pltpu.roll(x, 1, 0)
    row = lax.broadcasted_iota(jnp.int32, x.shape, 0)
    return jnp.where(row == 0, carry_row, rolled)


def _mixer_kernel(x0_ref, s0_ref, xn_ref, sn_ref, lx_ref, lg_ref,
                  vec_ref, mus_ref, w2_ref, a2_ref, g2_ref, lvec_ref, wbd_ref,
                  o_ref, ol_ref,
                  prev_ref, hs_ref, ab_ref, bt_ref, kt_ref, rb_ref, vv_ref, bh_ref, kh_ref,
                  pc_ref, bon_ref, gg_ref, y_ref, pw_ref, tf_ref, aak_ref, arb_ref, ark_ref,
                  ext_ref, hc_ref, yl_ref, ssq_ref,
                  *, chunks_per_seq, n_steps):
    j = pl.program_id(0)
    width = o_ref.shape[1]
    n_slabs = width // SLAB
    heads_per_slab = SLAB // HEAD_DIM

    vec = vec_ref[...]
    w0, a0, k_k, k_a, r_k, ln_w, ln_b, mu_r, mu_k, mu_v = (vec[i:i + 1, :] for i in range(10))
    mu_s = mus_ref[...]

    lane_s = lax.broadcasted_iota(jnp.int32, (SLAB, SLAB), 1)
    sub_s = lax.broadcasted_iota(jnp.int32, (SLAB, SLAB), 0)
    strict = lane_s < sub_s
    incl = lane_s <= sub_s
    eye = jnp.where(lane_s == sub_s, 1.0, 0.0).astype(F32)
    ones_bd = jnp.where((lane_s // HEAD_DIM) == (sub_s // HEAD_DIM), 1.0, 0.0).astype(BF16)
    head_of_lane = lax.broadcasted_iota(jnp.int32, (CHUNK, SLAB), 1) // HEAD_DIM
    row_c = lax.broadcasted_iota(jnp.int32, (CHUNK, SLAB), 0)
    cols = [slice(q * SLAB, (q + 1) * SLAB) for q in range(n_slabs)]
    slabs = range(n_slabs)
    n_sq = CHUNK.bit_length() - 2
    halves = (slice(0, CHUNK), slice(CHUNK, 2 * CHUNK))

    def shifted(x, off, mu, keep_carry):
        n = x.shape[1]
        if keep_carry is False:
            carry = jnp.zeros((1, n), F32)
        else:
            carry = prev_ref[0:1, off:off + n]
            if keep_carry is not None:
                carry = jnp.where(keep_carry, carry, 0.0)
        prev_ref[0:1, off:off + n] = x[CHUNK - 1:CHUNK, :]
        return x + (_shift_rows(x, carry) - x) * mu

    def prep_ops(x_ref, s_ref, rows, slot, keep_carry):
        small = {}

        def lora_inputs():
            sm = shifted(s_ref[rows, :].astype(F32), 3 * width, mu_s, keep_carry)
            s1 = sm[:, 0:LANES]
            s2 = sm[:, LANES:LANES + g2_ref.shape[0]]
            small["tanh_w"] = jnp.tanh(s1).astype(BF16)
            small["lin_a"] = s1.astype(BF16)
            small["sig_g"] = _sigmoid(s2).astype(BF16)

        val = {}

        def shift(q):
            lo = q * SLAB
            c = cols[q]
            val[q, "r"] = shifted(x_ref[rows, lo:lo + SLAB].astype(F32), lo, mu_r[:, c], keep_carry)
            val[q, "k"] = shifted(x_ref[rows, width + lo:width + lo + SLAB].astype(F32), width + lo,
                                  mu_k[:, c], keep_carry)
            v = shifted(x_ref[rows, 2 * width + lo:2 * width + lo + SLAB].astype(F32), 2 * width + lo,
                        mu_v[:, c], keep_carry)
            vv_ref[slot, :, c] = v.astype(BF16)
            val[q, "v"] = v

        def lora(q):
            c = cols[q]
            w_log = -_softplus(-(w0[:, c] + _dot(small["tanh_w"], w2_ref[:, c]))) - 0.5
            val[q, "a"] = _sigmoid(a0[:, c] + _dot(small["lin_a"], a2_ref[:, c]))
            gg_ref[slot, :, c] = _dot(small["sig_g"], g2_ref[:, c])
            val[q, "ld"] = -jnp.exp(w_log)

        def keys(q):
            c = cols[q]
            k = val[q, "k"]
            kk = k * k_k[:, c]
            val[q, "kk"] = kk * lax.rsqrt(jnp.maximum(_head_sum(kk * kk, ones_bd, exact=True), 1e-24))
            val[q, "k"] = k * (1.0 + (val[q, "a"] - 1.0) * k_a[:, c])

        def bonus(q):
            c = cols[q]
            rk = val[q, "r"] * val[q, "k"] * r_k[:, c]
            bon_ref[slot, :, c] = _head_sum(rk, ones_bd, exact=False) * val.pop((q, "v"))

        def decay(q):
            cum = val[q, "ld"]
            sh = 1
            while sh < CHUNK:
                cum = cum + jnp.where(row_c >= sh, pltpu.roll(cum, sh, 0), 0.0)
                sh *= 2
            val[q, "cum"] = cum

        def scale_in(q):
            c = cols[q]
            cum, ld, kk, r = val[q, "cum"], val.pop((q, "ld")), val[q, "kk"], val.pop((q, "r"))
            ab_ref[slot, :, c] = (-kk * jnp.exp(cum - ld)).astype(BF16)
            rb_ref[slot, :, c] = (r * jnp.exp(cum)).astype(BF16)

        def scale_out(q):
            c = cols[q]
            cum, kk, a, k = val.pop((q, "cum")), val.pop((q, "kk")), val.pop((q, "a")), val.pop((q, "k"))
            tot = cum[CHUNK - 1:CHUNK, :]
            kb = kk * a
            e_inv = jnp.exp(-cum)
            e_end = jnp.exp(tot - cum)
            bt_ref[slot, :, c] = (kb * e_inv).astype(BF16)
            kt_ref[slot, :, c] = (k * e_inv).astype(BF16)
            bh_ref[slot, :, c] = (kb * e_end).astype(BF16)
            kh_ref[slot, :, c] = (k * e_end).astype(BF16)
            pc_ref[slot, 0:1, c] = jnp.exp(tot)

        stages = (shift, lora, keys, bonus, decay, scale_in, scale_out)
        return [lora_inputs] + [functools.partial(stage, q) for q in slabs for stage in stages]

    def stack(x):
        zero = jnp.zeros_like(x)
        return jnp.concatenate([jnp.where(head_of_lane == hd, x, zero) for hd in range(heads_per_slab)],
                               axis=0)

    def chain_ops(slot, half, ybuf):
        live = {}

        def gram(q):
            xx = jnp.concatenate([stack(ab_ref[slot, :, cols[q]]), stack(rb_ref[slot, :, cols[q]])], axis=0)
            gb = _dot_nt(xx, stack(bt_ref[slot, :, cols[q]]))
            gk = _dot_nt(xx, stack(kt_ref[slot, :, cols[q]]))
            a_ab = jnp.where(strict, gb[:SLAB], 0.0)
            pw_ref[half, 0, q] = a_ab.astype(BF16)
            tf_ref[half, q] = eye + a_ab
            aak_ref[half, q] = jnp.where(strict, gk[:SLAB], 0.0).astype(BF16)
            arb_ref[half, q] = jnp.where(incl, gb[SLAB:], 0.0).astype(BF16)
            ark_ref[half, q] = jnp.where(incl, gk[SLAB:], 0.0).astype(BF16)

        def square(s, q):
            pw_ref[half, (s + 1) % 2, q] = _dot(pw_ref[half, s % 2, q], pw_ref[half, s % 2, q]).astype(BF16)

        def extend(s, q):
            tf_ref[half, q] = tf_ref[half, q] + _dot(tf_ref[half, q].astype(BF16), pw_ref[half, (s + 1) % 2, q])

        def project(q):
            vs = stack(vv_ref[slot, :, cols[q]])
            xx = jnp.concatenate([stack(ab_ref[slot, :, cols[q]]), stack(rb_ref[slot, :, cols[q]])], axis=0)
            live[q] = (vs, _dot_nt(xx, hs_ref[q].astype(BF16)))

        def rhs(q):
            vs, xh = live[q]
            live[q] = (vs, xh[SLAB:], (xh[:SLAB] + _dot(aak_ref[half, q], vs)).astype(BF16))

        def solve(q):
            vs, xrh, b = live[q]
            live[q] = (vs, xrh, _dot(tf_ref[half, q].astype(BF16), b).astype(BF16))

        def emit(q):
            vs, xrh, u = live[q]
            y_st = xrh + _dot(arb_ref[half, q], u) + _dot(ark_ref[half, q], vs)
            y = y_st[0:CHUNK]
            for hd in range(1, heads_per_slab):
                y = y + y_st[hd * CHUNK:(hd + 1) * CHUNK]
            y_ref[ybuf, half, :, cols[q]] = y

        def advance(q):
            vs, _, u = live.pop(q)
            pc = pc_ref[slot, 0:1, cols[q]]
            hs_ref[q] = (hs_ref[q] * pc + _dot_tn(u, stack(bh_ref[slot, :, cols[q]]))
                         + _dot_tn(vs, stack(kh_ref[slot, :, cols[q]])))

        free = [gram]
        for s in range(n_sq):
            free += [functools.partial(square, s), functools.partial(extend, s)]
        bound = [project, rhs, solve, emit, advance]
        per_slab = lambda stages: [[functools.partial(stage, q) for q in slabs] for stage in stages]
        return per_slab(free), per_slab(bound)

    def post_ops(slot, half, ybuf):
        val = {}
        inv_n = 1.0 / HEAD_DIM

        def center(q):
            y = y_ref[ybuf, half, :, cols[q]]
            val[q] = y - _head_sum(y, ones_bd, exact=False) * inv_n

        def scale(q):
            c = cols[q]
            yc = val.pop(q)
            var = _head_sum(yc * yc, ones_bd, exact=False) * inv_n
            yn = yc * lax.rsqrt(var + GN_EPS) * ln_w[:, c] + ln_b[:, c]
            o_ref[halves[half], c] = ((yn + bon_ref[slot, :, c]) * gg_ref[slot, :, c]).astype(o_ref.dtype)

        return [functools.partial(stage, q) for q in slabs for stage in (center, scale)]

    def lru_ops():
        n_rows = 2 * CHUNK
        pad = 8
        lvec = lvec_ref[...]
        conv_w = [lvec[i:i + 1, :] for i in range(CONV_WIDTH)]
        conv_b, b_a, b_x, lam, norm_g = (lvec[i:i + 1, :] for i in range(CONV_WIDTH, CONV_WIDTH + 5))
        row8 = lax.broadcasted_iota(jnp.int32, (8, LANES), 0)
        lane_cols = [slice(s * LANES, (s + 1) * LANES) for s in range(width // LANES)]
        val = {}

        def gates(s):
            ln = lane_cols[s]
            x = lx_ref[:, ln].astype(F32)
            ext_ref[pad:pad + n_rows, ln] = x
            xc = conv_b[:, ln] + conv_w[CONV_WIDTH - 1][:, ln] * x
            for i in range(CONV_WIDTH - 1):
                xc = xc + conv_w[i][:, ln] * ext_ref[pl.ds(pad - (CONV_WIDTH - 1) + i, n_rows), ln]
            ext_ref[0:pad, ln] = ext_ref[n_rows:n_rows + pad, ln]
            z = _dot(xc.astype(BF16), wbd_ref[s])
            r = _sigmoid(z[:, :LANES] + b_a[:, ln])
            i_gate = _sigmoid(z[:, LANES:] + b_x[:, ln])
            log_a = (-LRU_C) * r * _softplus(-lam[:, ln])
            a = jnp.exp(log_a)
            val[s] = (a, jnp.sqrt(-jnp.tanh(log_a) * (a * a + 1.0)) * (i_gate * xc))

        n_groups = n_rows // 8
        scan_parts = 4

        def scan(s, part):
            ln = lane_cols[s]
            a, u = val[s]
            if part == 0:
                val[s, "hc"] = hc_ref[:, ln]
                val[s, "h"] = []
            hc = val[s, "hc"]
            for g in range(part * n_groups // scan_parts, (part + 1) * n_groups // scan_parts):
                aa, uu = a[8 * g:8 * g + 8], u[8 * g:8 * g + 8]
                for sh in (1, 2, 4):
                    ok = row8 >= sh
                    uu = jnp.where(ok, aa * pltpu.roll(uu, sh, 0) + uu, uu)
                    aa = jnp.where(ok, aa * pltpu.roll(aa, sh, 0), aa)
                hh = aa * hc + uu
                val[s, "h"].append(hh)
                hc = jnp.broadcast_to(hh[7:8, :], (8, LANES))
            val[s, "hc"] = hc
            if part == scan_parts - 1:
                hc_ref[:, ln] = val.pop((s, "hc"))
                del val[s]

        def gate(s):
            ln = lane_cols[s]
            gt = lg_ref[:, ln].astype(F32)
            gelu = 0.5 * gt * (1.0 + jnp.tanh(0.7978845608028654 * (gt + 0.044715 * gt * gt * gt)))
            y = jnp.concatenate(val.pop((s, "h")), axis=0) * gelu
            yl_ref[:, ln] = y
            ssq_ref[...] = y * y if s == 0 else ssq_ref[...] + y * y

        def finish():
            ms = jnp.sum(ssq_ref[...], axis=-1, keepdims=True) * (1.0 / width)
            scale = lax.rsqrt(ms + NORM_EPS)
            for ln in lane_cols:
                ol_ref[:, ln] = (yl_ref[:, ln] * scale * norm_g[:, ln]).astype(ol_ref.dtype)

        ops = []
        for s in range(len(lane_cols)):
            ops += [functools.partial(gates, s)]
            ops += [functools.partial(scan, s, part) for part in range(scan_parts)]
            ops += [functools.partial(gate, s)]
        return ops + [finish]

    def interleave(main, fill):
        n, m = len(main), len(fill)
        done = 0
        for i, op in enumerate(main):
            op()
            while done < m and (done + 1) * n <= (i + 1) * (m + 1):
                fill[done]()
                done += 1
        for op in fill[done:]:
            op()

    slot_sets = ((0, 1), (2, 3))

    def step(parity):
        chain_slots, prep_slots = slot_sets[parity], slot_sets[1 - parity]
        keep_next = (2 * j + 2) % chunks_per_seq != 0
        free0, bound0 = chain_ops(chain_slots[0], 0, parity)
        free1, bound1 = chain_ops(chain_slots[1], 1, parity)
        main = [op for pair in zip(free0, free1) for stage in pair for op in stage]
        main += [op for stage in bound0 + bound1 for op in stage]
        lru = lru_ops()
        fill = []
        for half in range(2):
            fill += post_ops(prep_slots[half], half, 1 - parity)
            fill += prep_ops(xn_ref, sn_ref, halves[half], prep_slots[half], keep_next if half == 0 else None)
            fill += lru[half * len(lru) // 2:(half + 1) * len(lru) // 2]
        interleave(main, fill)

    @pl.when(j == 0)
    def _():
        y_ref[1] = jnp.zeros_like(y_ref[1])
        for slot in slot_sets[1]:
            bon_ref[slot] = jnp.zeros_like(bon_ref[slot])
            gg_ref[slot] = jnp.zeros_like(gg_ref[slot])
        for op in prep_ops(x0_ref, s0_ref, halves[0], 0, False) + prep_ops(x0_ref, s0_ref, halves[1], 1, None):
            op()

    @pl.when((2 * j) % chunks_per_seq == 0)
    def _():
        hs_ref[...] = jnp.zeros_like(hs_ref)
        hc_ref[...] = jnp.zeros_like(hc_ref)
        ext_ref[0:8, :] = jnp.zeros((8, width), F32)

    for parity in range(2):
        @pl.when((j % 2 == parity) & (j < n_steps))
        def _():
            step(parity)

    @pl.when(j == n_steps)
    def _():
        last = n_steps % 2
        for half in range(2):
            for op in post_ops(slot_sets[1 - last][half], half, 1 - last):
                op()


def _mixers(p, vec, mus, w2p, a2p, g2p, lvec, w_bd, *, width, lru_x_col, lru_gate_col, small_col, small_w,
            chunks_per_seq):
    t = p.shape[0]
    n_chunks = t // CHUNK
    assert chunks_per_seq % 2 == 0 and n_chunks % chunks_per_seq == 0
    n_slabs = width // SLAB
    n_steps = n_chunks // 2
    block = 2 * CHUNK
    views = []
    for idx in (lambda j: 0, lambda j: jnp.minimum(j + 1, n_steps - 1)):
        views.append(pl.BlockSpec((block, 3 * width), lambda j, idx=idx: (idx(j), 0)))
        views.append(pl.BlockSpec((block, small_w), lambda j, idx=idx: (idx(j), small_col)))
    current = lambda j: jnp.minimum(j, n_steps - 1)
    views.append(pl.BlockSpec((block, width), lambda j: (current(j), lru_x_col)))
    views.append(pl.BlockSpec((block, width), lambda j: (current(j), lru_gate_col)))
    full = lambda a: pl.BlockSpec(a.shape, lambda j: (0,) * a.ndim)
    n_slots = 4
    slot_bf16 = pltpu.VMEM((n_slots, CHUNK, width), BF16)
    slot_f32 = pltpu.VMEM((n_slots, CHUNK, width), F32)
    return pl.pallas_call(
        functools.partial(_mixer_kernel, chunks_per_seq=chunks_per_seq, n_steps=n_steps),
        grid=(n_steps + 1,),
        in_specs=views + [full(vec), full(mus), full(w2p), full(a2p), full(g2p), full(lvec), full(w_bd)],
        out_specs=[pl.BlockSpec((block, width), lambda j: (jnp.maximum(j - 1, 0), 0)),
                   pl.BlockSpec((block, width), lambda j: (current(j), 0))],
        out_shape=[jax.ShapeDtypeStruct((t, width), BF16), jax.ShapeDtypeStruct((t, width), BF16)],
        scratch_shapes=[
            pltpu.VMEM((8, 3 * width + small_w), F32),
            pltpu.VMEM((n_slabs, SLAB, SLAB), F32),
            slot_bf16, slot_bf16, slot_bf16, slot_bf16, slot_bf16, slot_bf16, slot_bf16,
            pltpu.VMEM((n_slots, 8, width), F32),
            slot_f32, slot_f32,
            pltpu.VMEM((2, 2, CHUNK, width), F32),
            pltpu.VMEM((2, 2, n_slabs, SLAB, SLAB), BF16),
            pltpu.VMEM((2, n_slabs, SLAB, SLAB), F32),
            pltpu.VMEM((2, n_slabs, SLAB, SLAB), BF16), pltpu.VMEM((2, n_slabs, SLAB, SLAB), BF16),
            pltpu.VMEM((2, n_slabs, SLAB, SLAB), BF16),
            pltpu.VMEM((block + 8, width), F32),
            pltpu.VMEM((8, width), F32),
            pltpu.VMEM((block, width), F32),
            pltpu.VMEM((block, LANES), F32),
        ],
        compiler_params=pltpu.CompilerParams(
            dimension_semantics=("arbitrary",), vmem_limit_bytes=VMEM_LIMIT),
        name="mixers",
    )(p, p, p, p, p, p, vec, mus, w2p, a2p, g2p, lvec, w_bd)


def _pad_rows(w, rows, at):
    out = jnp.zeros((rows, w.shape[1]), w.dtype)
    return out.at[at:at + w.shape[0]].set(w)


def _block_diag_pairs(wa, wx):
    def bd(w):
        h, n, _ = w.shape
        w = w.reshape(h // PAIR, PAIR, n, n)
        z = jnp.zeros_like(w[:, 0])
        top = jnp.concatenate([w[:, 0], z], axis=2)
        bot = jnp.concatenate([z, w[:, 1]], axis=2)
        return jnp.concatenate([top, bot], axis=1)
    return jnp.concatenate([bd(wa), bd(wx)], axis=2)


def kernel(x, ffn1_norm, ffn1_w_gate, ffn1_w_up, ffn1_w_down, mix_norm, w_in, rwkv_mu, rwkv_w0, rwkv_w2, rwkv_a0, rwkv_a2, rwkv_g2, rwkv_k_k, rwkv_k_a, rwkv_r_k, rwkv_ln_w, rwkv_ln_b, lru_conv_w, lru_conv_b, lru_wa, lru_ba, lru_wx, lru_bx, lru_lam, lru_norm, w_out, ffn2_norm, ffn2_w_gate, ffn2_w_up, ffn2_w_down, final_norm):
    bsz, seq, d = x.shape
    depth = w_in.shape[0]
    wr = rwkv_w0.shape[1]
    wl = lru_lam.shape[1]
    n_w, n_a, n_g = rwkv_w2.shape[1], rwkv_a2.shape[1], rwkv_g2.shape[1]
    assert wr == wl and wr % SLAB == 0 and n_w + n_a == LANES and lru_wa.shape[2] == HEAD_DIM
    small_w = 4 * LANES
    g_pad = 2 * LANES
    assert n_g <= g_pad and (3 * wr) % small_w == 0 and seq % (2 * CHUNK) == 0

    row = lambda v: v.reshape(1, -1).astype(F32)
    h = x.reshape(bsz * seq, d)
    for l in range(depth):
        h = _ffn(h, row(ffn1_norm[l]), ffn1_w_gate[l].astype(BF16), ffn1_w_up[l].astype(BF16),
                 ffn1_w_down[l].astype(BF16), row(final_norm), final_norm=False)

        o_r, o_w = 0, wr
        o_k = o_w + n_w
        o_v = o_k + wr
        o_a = o_v + wr
        o_g = o_a + n_a
        o_l = o_g + n_g
        w = w_in[l]
        zpad = jnp.zeros((d, small_w - n_w - n_a - n_g), w.dtype)
        w_re = jnp.concatenate([w[:, o_r:o_r + wr], w[:, o_k:o_k + wr], w[:, o_v:o_v + wr],
                                w[:, o_l:o_l + 2 * wl],
                                w[:, o_w:o_w + n_w], w[:, o_a:o_a + n_a], w[:, o_g:o_g + n_g], zpad],
                               axis=1).astype(BF16)
        mu = rwkv_mu[l]
        mus = jnp.concatenate([mu[o_w:o_w + n_w], mu[o_a:o_a + n_a], mu[o_g:o_g + n_g],
                               jnp.zeros((small_w - n_w - n_a - n_g,), F32)]).reshape(1, -1)
        vec = jnp.stack([rwkv_w0[l], rwkv_a0[l], rwkv_k_k[l], rwkv_k_a[l], rwkv_r_k[l].reshape(-1),
                         rwkv_ln_w[l], rwkv_ln_b[l],
                         mu[o_r:o_r + wr], mu[o_k:o_k + wr], mu[o_v:o_v + wr]]
                        + [jnp.zeros((wr,), F32)] * 6).astype(F32)
        w2p = _pad_rows(rwkv_w2[l], LANES, 0).astype(BF16)
        a2p = _pad_rows(rwkv_a2[l], LANES, n_w).astype(BF16)
        g2p = _pad_rows(rwkv_g2[l], g_pad, 0).astype(BF16)

        lvec = jnp.concatenate([lru_conv_w[l], jnp.stack([lru_conv_b[l], lru_ba[l], lru_bx[l],
                                                          lru_lam[l], lru_norm[l]]),
                                jnp.zeros((16 - CONV_WIDTH - 5, wl), F32)], axis=0).astype(F32)
        w_bd = _block_diag_pairs(lru_wa[l], lru_wx[l]).astype(BF16)

        p = _proj(h, row(mix_norm[l]), w_re)
        y_r, y_l = _mixers(p, vec, mus, w2p, a2p, g2p, lvec, w_bd, width=wr, lru_x_col=3, lru_gate_col=4,
                           small_col=(3 * wr + 2 * wl) // small_w, small_w=small_w,
                           chunks_per_seq=seq // CHUNK)
        h = _outproj(h, y_r, y_l, w_out[l].astype(BF16))
        h = _ffn(h, row(ffn2_norm[l]), ffn2_w_gate[l].astype(BF16), ffn2_w_up[l].astype(BF16),
                 ffn2_w_down[l].astype(BF16), row(final_norm), final_norm=(l == depth - 1))
    return h.reshape(bsz, seq, d)
```

```python
import functools

import jax
import jax.numpy as jnp
from jax import lax
from jax.experimental import pallas as pl
from jax.experimental.pallas import tpu as pltpu

F32 = jnp.float32
BF16 = jnp.bfloat16

HEAD_DIM = 64
CONV_WIDTH = 4
LRU_C = 8.0
NORM_EPS = 1e-6
DECAY_SCALE = 0.6065306597126334
GN_EPS = 64e-5

LANES = 128
CHUNK = 64
PAIR = LANES // HEAD_DIM
SLAB = 256
VMEM_PHYSICAL_V7X = 64 * 1024 * 1024
VMEM_LIMIT = VMEM_PHYSICAL_V7X - 4 * 1024 * 1024


def _dot(a, b):
    return jnp.dot(a, b, preferred_element_type=F32)


def _dot_nt(a, b):
    return lax.dot_general(a, b, (((1,), (1,)), ((), ())), preferred_element_type=F32)


def _dot_tn(a, b):
    return lax.dot_general(a, b, (((0,), (0,)), ((), ())), preferred_element_type=F32)


def _sigmoid(x):
    return 0.5 * (jnp.tanh(0.5 * x) + 1.0)


def _softplus(x):
    return jnp.maximum(x, 0.0) + jnp.log(1.0 + jnp.exp(-jnp.abs(x)))


def _rms_norm(x, g):
    ms = jnp.mean(x * x, axis=-1, keepdims=True)
    return x * lax.rsqrt(ms + NORM_EPS) * g


def _ffn_kernel(h_ref, g_ref, wg_ref, wu_ref, wd_ref, fg_ref, o_ref, xn_ref, *, final_norm):
    j = pl.program_id(1)

    @pl.when(j == 0)
    def _():
        xn_ref[...] = _rms_norm(h_ref[...], g_ref[...]).astype(BF16)
        o_ref[...] = jnp.zeros_like(o_ref)

    xn = xn_ref[...]
    gate = _dot(xn, wg_ref[...])
    up = _dot(xn, wu_ref[...])
    act = (gate * _sigmoid(gate) * up).astype(BF16)
    o_ref[...] += _dot(act, wd_ref[...])

    @pl.when(j == pl.num_programs(1) - 1)
    def _():
        out = h_ref[...] + 0.5 * o_ref[...]
        if final_norm:
            out = _rms_norm(out, fg_ref[...])
        o_ref[...] = out


def _ffn(h, g, wg, wu, wd, fg, *, final_norm, tf=512):
    t, d = h.shape
    tm = 512 if final_norm else 1024
    dff = wg.shape[1]
    grid = (t // tm, dff // tf)
    return pl.pallas_call(
        functools.partial(_ffn_kernel, final_norm=final_norm),
        grid=grid,
        in_specs=[
            pl.BlockSpec((tm, d), lambda i, j: (i, 0)),
            pl.BlockSpec((1, d), lambda i, j: (0, 0)),
            pl.BlockSpec((d, tf), lambda i, j: (0, j)),
            pl.BlockSpec((d, tf), lambda i, j: (0, j)),
            pl.BlockSpec((tf, d), lambda i, j: (j, 0)),
            pl.BlockSpec((1, d), lambda i, j: (0, 0)),
        ],
        out_specs=pl.BlockSpec((tm, d), lambda i, j: (i, 0)),
        out_shape=jax.ShapeDtypeStruct((t, d), F32),
        scratch_shapes=[pltpu.VMEM((tm, d), BF16)],
        compiler_params=pltpu.CompilerParams(
            dimension_semantics=("parallel", "arbitrary"), vmem_limit_bytes=VMEM_LIMIT),
        name="ffn",
    )(h, g, wg, wu, wd, fg)


def _proj_kernel(h_ref, g_ref, w_ref, mu_ref, o_ref, xn_ref, last_ref, *, tiles_per_seq):
    i, j = pl.program_id(0), pl.program_id(1)

    @pl.when(j == 0)
    def _():
        xn_ref[...] = _rms_norm(h_ref[...], g_ref[...]).astype(BF16)

    @pl.when(i == 0)
    def _():
        last_ref[j] = jnp.zeros(last_ref.shape[1:], F32)

    z = _dot(xn_ref[...], w_ref[...])
    above = jnp.where(i % tiles_per_seq == 0, 0.0, last_ref[j, 0:1, :])
    last_ref[j, 0:1, :] = z[z.shape[0] - 1:, :]
    o_ref[...] = (z + (_shift_rows(z, above) - z) * mu_ref[...]).astype(o_ref.dtype)


def _proj(h, g, w, mu, *, rows_per_seq, tm=512, tn=2816):
    t, d = h.shape
    n = w.shape[1]
    assert rows_per_seq % tm == 0
    return pl.pallas_call(
        functools.partial(_proj_kernel, tiles_per_seq=rows_per_seq // tm),
        grid=(t // tm, n // tn),
        in_specs=[
            pl.BlockSpec((tm, d), lambda i, j: (i, 0)),
            pl.BlockSpec((1, d), lambda i, j: (0, 0)),
            pl.BlockSpec((d, tn), lambda i, j: (0, j)),
            pl.BlockSpec((1, tn), lambda i, j: (0, j)),
        ],
        out_specs=pl.BlockSpec((tm, tn), lambda i, j: (i, j)),
        out_shape=jax.ShapeDtypeStruct((t, n), BF16),
        scratch_shapes=[pltpu.VMEM((tm, d), BF16),
                        pltpu.VMEM((n // tn, 8, tn), F32)],
        compiler_params=pltpu.CompilerParams(
            dimension_semantics=("arbitrary", "arbitrary"), vmem_limit_bytes=VMEM_LIMIT),
        name="proj",
    )(h, g, w, mu)


def _outproj_kernel(h_ref, ya_ref, yb_ref, w_ref, o_ref):
    ka = ya_ref.shape[1]
    o_ref[...] = (h_ref[...] + _dot(ya_ref[...], w_ref[:ka, :]) + _dot(yb_ref[...], w_ref[ka:, :]))


def _outproj(h, ya, yb, w, *, tm=512):
    t, d = h.shape
    ka, kb = ya.shape[1], yb.shape[1]
    return pl.pallas_call(
        _outproj_kernel,
        grid=(t // tm,),
        in_specs=[
            pl.BlockSpec((tm, d), lambda i: (i, 0)),
            pl.BlockSpec((tm, ka), lambda i: (i, 0)),
            pl.BlockSpec((tm, kb), lambda i: (i, 0)),
            pl.BlockSpec((ka + kb, d), lambda i: (0, 0)),
        ],
        out_specs=pl.BlockSpec((tm, d), lambda i: (i, 0)),
        out_shape=jax.ShapeDtypeStruct((t, d), F32),
        compiler_params=pltpu.CompilerParams(
            dimension_semantics=("parallel",), vmem_limit_bytes=VMEM_LIMIT),
        name="outproj",
    )(h, ya, yb, w)


def _head_sum(x, ones_bd, exact):
    hi = x.astype(BF16)
    if not exact:
        return _dot(hi, ones_bd)
    r = x.shape[0]
    lo = (x - hi.astype(F32)).astype(BF16)
    z = _dot(jnp.concatenate([hi, lo], axis=0), ones_bd)
    return z[:r] + z[r:]


def _shift_rows(x, carry_row):
    rolled = pltpu.roll(x, 1, 0)
    row = lax.broadcasted_iota(jnp.int32, x.shape, 0)
    return jnp.where(row == 0, carry_row, rolled)


def _mixer_kernel(x0_ref, s0_ref, xn_ref, sn_ref, lx_ref, lg_ref,
                  vec_ref, w2_ref, a2_ref, g2_ref, lvec_ref, wbd_ref,
                  o_ref, ol_ref,
                  hs_ref, ab_ref, bt_ref, kt_ref, rb_ref, vv_ref, bh_ref, kh_ref,
                  pc_ref, bon_ref, gg_ref, y_ref, pw_ref, tf_ref, aak_ref, arb_ref, ark_ref,
                  ext_ref, hc_ref, yl_ref, ssq_ref,
                  *, chunks_per_seq, n_steps):
    j = pl.program_id(0)
    width = o_ref.shape[1]
    n_slabs = width // SLAB
    heads_per_slab = SLAB // HEAD_DIM

    vec = vec_ref[...]
    w0, a0, k_k, k_a, r_k, ln_w, ln_b = (vec[i:i + 1, :] for i in range(7))

    lane_s = lax.broadcasted_iota(jnp.int32, (SLAB, SLAB), 1)
    sub_s = lax.broadcasted_iota(jnp.int32, (SLAB, SLAB), 0)
    strict = lane_s < sub_s
    incl = lane_s <= sub_s
    eye = jnp.where(lane_s == sub_s, 1.0, 0.0).astype(F32)
    ones_bd = jnp.where((lane_s // HEAD_DIM) == (sub_s // HEAD_DIM), 1.0, 0.0).astype(BF16)
    head_of_lane = lax.broadcasted_iota(jnp.int32, (CHUNK, SLAB), 1) // HEAD_DIM
    row_c = lax.broadcasted_iota(jnp.int32, (CHUNK, SLAB), 0)
    cols = [slice(q * SLAB, (q + 1) * SLAB) for q in range(n_slabs)]
    slabs = range(n_slabs)
    n_sq = CHUNK.bit_length() - 2
    halves = (slice(0, CHUNK), slice(CHUNK, 2 * CHUNK))

    def prep_ops(x_ref, s_ref, rows, slot):
        small = {}

        def lora_inputs():
            s1 = s_ref[rows, 0:LANES].astype(F32)
            s2 = s_ref[rows, LANES:LANES + g2_ref.shape[0]].astype(F32)
            small["tanh_w"] = jnp.tanh(s1).astype(BF16)
            small["lin_a"] = s_ref[rows, 0:LANES]
            small["sig_g"] = _sigmoid(s2).astype(BF16)

        val = {}

        def load(q):
            lo = q * SLAB
            val[q, "r"] = x_ref[rows, lo:lo + SLAB].astype(F32)
            val[q, "k"] = x_ref[rows, width + lo:width + lo + SLAB].astype(F32)
            vv_ref[slot, :, cols[q]] = x_ref[rows, 2 * width + lo:2 * width + lo + SLAB]
            val[q, "v"] = x_ref[rows, 2 * width + lo:2 * width + lo + SLAB].astype(F32)

        def lora(q):
            c = cols[q]
            z = w0[:, c] + _dot(small["tanh_w"], w2_ref[:, c])
            val[q, "a"] = _sigmoid(a0[:, c] + _dot(small["lin_a"], a2_ref[:, c]))
            gg_ref[slot, :, c] = _dot(small["sig_g"], g2_ref[:, c])
            val[q, "ld"] = (-DECAY_SCALE) * _sigmoid(z)

        def keys(q):
            c = cols[q]
            k = val[q, "k"]
            kk = k * k_k[:, c]
            val[q, "kk"] = kk * lax.rsqrt(jnp.maximum(_head_sum(kk * kk, ones_bd, exact=True), 1e-24))
            val[q, "k"] = k * (1.0 + (val[q, "a"] - 1.0) * k_a[:, c])

        def bonus(q):
            c = cols[q]
            rk = val[q, "r"] * val[q, "k"] * r_k[:, c]
            bon_ref[slot, :, c] = _head_sum(rk, ones_bd, exact=False) * val.pop((q, "v"))

        def decay(q):
            cum = val[q, "ld"]
            sh = 1
            while sh < CHUNK:
                cum = cum + jnp.where(row_c >= sh, pltpu.roll(cum, sh, 0), 0.0)
                sh *= 2
            val[q, "cum"] = cum

        def scale_in(q):
            c = cols[q]
            cum, ld, kk, r = val[q, "cum"], val.pop((q, "ld")), val[q, "kk"], val.pop((q, "r"))
            ab_ref[slot, :, c] = (-kk * jnp.exp(cum - ld)).astype(BF16)
            rb_ref[slot, :, c] = (r * jnp.exp(cum)).astype(BF16)

        def scale_out(q):
            c = cols[q]
            cum, kk, a, k = val.pop((q, "cum")), val.pop((q, "kk")), val.pop((q, "a")), val.pop((q, "k"))
            tot = cum[CHUNK - 1:CHUNK, :]
            kb = kk * a
            e_inv = jnp.exp(-cum)
            e_end = jnp.exp(tot - cum)
            bt_ref[slot, :, c] = (kb * e_inv).astype(BF16)
            kt_ref[slot, :, c] = (k * e_inv).astype(BF16)
            bh_ref[slot, :, c] = (kb * e_end).astype(BF16)
            kh_ref[slot, :, c] = (k * e_end).astype(BF16)
            pc_ref[slot, 0:1, c] = jnp.exp(tot)

        stages = (load, lora, keys, bonus, decay, scale_in, scale_out)
        return [lora_inputs] + [functools.partial(stage, q) for q in slabs for stage in stages]

    def stack(x):
        zero = jnp.zeros_like(x)
        return jnp.concatenate([jnp.where(head_of_lane == hd, x, zero) for hd in range(heads_per_slab)],
                               axis=0)

    def chain_ops(slot, half, ybuf):
        live = {}

        def gram(q):
            xx = jnp.concatenate([stack(ab_ref[slot, :, cols[q]]), stack(rb_ref[slot, :, cols[q]])], axis=0)
            gb = _dot_nt(xx, stack(bt_ref[slot, :, cols[q]]))
            gk = _dot_nt(xx, stack(kt_ref[slot, :, cols[q]]))
            a_ab = jnp.where(strict, gb[:SLAB], 0.0)
            pw_ref[half, 0, q] = a_ab.astype(BF16)
            tf_ref[half, q] = eye + a_ab
            aak_ref[half, q] = jnp.where(strict, gk[:SLAB], 0.0).astype(BF16)
            arb_ref[half, q] = jnp.where(incl, gb[SLAB:], 0.0).astype(BF16)
            ark_ref[half, q] = jnp.where(incl, gk[SLAB:], 0.0).astype(BF16)

        def square(s, q):
            pw_ref[half, (s + 1) % 2, q] = _dot(pw_ref[half, s % 2, q], pw_ref[half, s % 2, q]).astype(BF16)

        def extend(s, q):
            tf_ref[half, q] = tf_ref[half, q] + _dot(tf_ref[half, q].astype(BF16), pw_ref[half, (s + 1) % 2, q])

        def project(q):
            vs = stack(vv_ref[slot, :, cols[q]])
            xx = jnp.concatenate([stack(ab_ref[slot, :, cols[q]]), stack(rb_ref[slot, :, cols[q]])], axis=0)
            live[q] = (vs, _dot_nt(xx, hs_ref[q].astype(BF16)))

        def rhs(q):
            vs, xh = live[q]
            live[q] = (vs, xh[SLAB:], (xh[:SLAB] + _dot(aak_ref[half, q], vs)).astype(BF16))

        def solve(q):
            vs, xrh, b = live[q]
            live[q] = (vs, xrh, _dot(tf_ref[half, q].astype(BF16), b).astype(BF16))

        def emit(q):
            vs, xrh, u = live[q]
            y_st = xrh + _dot(arb_ref[half, q], u) + _dot(ark_ref[half, q], vs)
            y = y_st[0:CHUNK]
            for hd in range(1, heads_per_slab):
                y = y + y_st[hd * CHUNK:(hd + 1) * CHUNK]
            y_ref[ybuf, half, :, cols[q]] = y

        def advance(q):
            vs, _, u = live.pop(q)
            pc = pc_ref[slot, 0:1, cols[q]]
            hs_ref[q] = (hs_ref[q] * pc + _dot_tn(u, stack(bh_ref[slot, :, cols[q]]))
                         + _dot_tn(vs, stack(kh_ref[slot, :, cols[q]])))

        free = [gram]
        for s in range(n_sq):
            free += [functools.partial(square, s), functools.partial(extend, s)]
        bound = [project, rhs, solve, emit, advance]
        per_slab = lambda stages: [[functools.partial(stage, q) for q in slabs] for stage in stages]
        return per_slab(free), per_slab(bound)

    def post_ops(slot, half, ybuf):
        val = {}
        inv_n = 1.0 / HEAD_DIM

        def center(q):
            y = y_ref[ybuf, half, :, cols[q]]
            val[q] = y - _head_sum(y, ones_bd, exact=False) * inv_n

        def scale(q):
            c = cols[q]
            yc = val.pop(q)
            var = _head_sum(yc * yc, ones_bd, exact=False) * inv_n
            yn = yc * lax.rsqrt(var + GN_EPS) * ln_w[:, c] + ln_b[:, c]
            o_ref[halves[half], c] = ((yn + bon_ref[slot, :, c]) * gg_ref[slot, :, c]).astype(o_ref.dtype)

        return [functools.partial(stage, q) for q in slabs for stage in (center, scale)]

    def lru_ops():
        n_rows = 2 * CHUNK
        pad = 8
        lvec = lvec_ref[...]
        conv_w = [lvec[i:i + 1, :] for i in range(CONV_WIDTH)]
        conv_b, b_a, b_x, lam, norm_g = (lvec[i:i + 1, :] for i in range(CONV_WIDTH, CONV_WIDTH + 5))
        row8 = lax.broadcasted_iota(jnp.int32, (8, LANES), 0)
        lane_cols = [slice(s * LANES, (s + 1) * LANES) for s in range(width // LANES)]
        val = {}

        def gates(s):
            ln = lane_cols[s]
            x = lx_ref[:, ln].astype(F32)
            ext_ref[pad:pad + n_rows, ln] = x
            xc = conv_b[:, ln] + conv_w[CONV_WIDTH - 1][:, ln] * x
            for i in range(CONV_WIDTH - 1):
                xc = xc + conv_w[i][:, ln] * ext_ref[pl.ds(pad - (CONV_WIDTH - 1) + i, n_rows), ln]
            ext_ref[0:pad, ln] = ext_ref[n_rows:n_rows + pad, ln]
            z = _dot(xc.astype(BF16), wbd_ref[s])
            r = _sigmoid(z[:, :LANES] + b_a[:, ln])
            i_gate = _sigmoid(z[:, LANES:] + b_x[:, ln])
            log_a = (-LRU_C) * r * _softplus(-lam[:, ln])
            a = jnp.exp(log_a)
            val[s] = (a, jnp.sqrt(-jnp.tanh(log_a) * (a * a + 1.0)) * (i_gate * xc))

        n_groups = n_rows // 8
        scan_parts = 4

        def scan(s, part):
            ln = lane_cols[s]
            a, u = val[s]
            if part == 0:
                val[s, "hc"] = hc_ref[:, ln]
                val[s, "h"] = []
            hc = val[s, "hc"]
            for g in range(part * n_groups // scan_parts, (part + 1) * n_groups // scan_parts):
                aa, uu = a[8 * g:8 * g + 8], u[8 * g:8 * g + 8]
                for sh in (1, 2, 4):
                    ok = row8 >= sh
                    uu = jnp.where(ok, aa * pltpu.roll(uu, sh, 0) + uu, uu)
                    aa = jnp.where(ok, aa * pltpu.roll(aa, sh, 0), aa)
                hh = aa * hc + uu
                val[s, "h"].append(hh)
                hc = jnp.broadcast_to(hh[7:8, :], (8, LANES))
            val[s, "hc"] = hc
            if part == scan_parts - 1:
                hc_ref[:, ln] = val.pop((s, "hc"))
                del val[s]

        def gate(s):
            ln = lane_cols[s]
            gt = lg_ref[:, ln].astype(F32)
            gelu = 0.5 * gt * (1.0 + jnp.tanh(0.7978845608028654 * (gt + 0.044715 * gt * gt * gt)))
            y = jnp.concatenate(val.pop((s, "h")), axis=0) * gelu
            yl_ref[:, ln] = y
            ssq_ref[...] = y * y if s == 0 else ssq_ref[...] + y * y

        def finish():
            ms = jnp.sum(ssq_ref[...], axis=-1, keepdims=True) * (1.0 / width)
            scale = lax.rsqrt(ms + NORM_EPS)
            for ln in lane_cols:
                ol_ref[:, ln] = (yl_ref[:, ln] * scale * norm_g[:, ln]).astype(ol_ref.dtype)

        ops = []
        for s in range(len(lane_cols)):
            ops += [functools.partial(gates, s)]
            ops += [functools.partial(scan, s, part) for part in range(scan_parts)]
            ops += [functools.partial(gate, s)]
        return ops + [finish]

    def interleave(main, fill):
        n, m = len(main), len(fill)
        done = 0
        for i, op in enumerate(main):
            op()
            while done < m and (done + 1) * n <= (i + 1) * (m + 1):
                fill[done]()
                done += 1
        for op in fill[done:]:
            op()

    slot_sets = ((0, 1), (2, 3))

    def step(parity):
        chain_slots, prep_slots = slot_sets[parity], slot_sets[1 - parity]
        free0, bound0 = chain_ops(chain_slots[0], 0, parity)
        free1, bound1 = chain_ops(chain_slots[1], 1, parity)
        main = [op for pair in zip(free0, free1) for stage in pair for op in stage]
        main += [op for stage in bound0 + bound1 for op in stage]
        lru = lru_ops()
        fill = []
        for half in range(2):
            fill += post_ops(prep_slots[half], half, 1 - parity)
            fill += prep_ops(xn_ref, sn_ref, halves[half], prep_slots[half])
            fill += lru[half * len(lru) // 2:(half + 1) * len(lru) // 2]
        interleave(main, fill)

    @pl.when(j == 0)
    def _():
        y_ref[1] = jnp.zeros_like(y_ref[1])
        for slot in slot_sets[1]:
            bon_ref[slot] = jnp.zeros_like(bon_ref[slot])
            gg_ref[slot] = jnp.zeros_like(gg_ref[slot])
        for op in prep_ops(x0_ref, s0_ref, halves[0], 0) + prep_ops(x0_ref, s0_ref, halves[1], 1):
            op()

    @pl.when((2 * j) % chunks_per_seq == 0)
    def _():
        hs_ref[...] = jnp.zeros_like(hs_ref)
        hc_ref[...] = jnp.zeros_like(hc_ref)
        ext_ref[0:8, :] = jnp.zeros((8, width), F32)

    for parity in range(2):
        @pl.when((j % 2 == parity) & (j < n_steps))
        def _():
            step(parity)

    @pl.when(j == n_steps)
    def _():
        last = n_steps % 2
        for half in range(2):
            for op in post_ops(slot_sets[1 - last][half], half, 1 - last):
                op()


def _mixers(p, vec, w2p, a2p, g2p, lvec, w_bd, *, width, lru_x_col, lru_gate_col, small_col, small_w,
            chunks_per_seq):
    t = p.shape[0]
    n_chunks = t // CHUNK
    assert chunks_per_seq % 2 == 0 and n_chunks % chunks_per_seq == 0
    n_slabs = width // SLAB
    n_steps = n_chunks // 2
    block = 2 * CHUNK
    views = []
    for idx in (lambda j: 0, lambda j: jnp.minimum(j + 1, n_steps - 1)):
        views.append(pl.BlockSpec((block, 3 * width), lambda j, idx=idx: (idx(j), 0)))
        views.append(pl.BlockSpec((block, small_w), lambda j, idx=idx: (idx(j), small_col)))
    current = lambda j: jnp.minimum(j, n_steps - 1)
    views.append(pl.BlockSpec((block, width), lambda j: (current(j), lru_x_col)))
    views.append(pl.BlockSpec((block, width), lambda j: (current(j), lru_gate_col)))
    full = lambda a: pl.BlockSpec(a.shape, lambda j: (0,) * a.ndim)
    n_slots = 4
    slot_bf16 = pltpu.VMEM((n_slots, CHUNK, width), BF16)
    slot_f32 = pltpu.VMEM((n_slots, CHUNK, width), F32)
    return pl.pallas_call(
        functools.partial(_mixer_kernel, chunks_per_seq=chunks_per_seq, n_steps=n_steps),
        grid=(n_steps + 1,),
        in_specs=views + [full(vec), full(w2p), full(a2p), full(g2p), full(lvec), full(w_bd)],
        out_specs=[pl.BlockSpec((block, width), lambda j: (jnp.maximum(j - 1, 0), 0)),
                   pl.BlockSpec((block, width), lambda j: (current(j), 0))],
        out_shape=[jax.ShapeDtypeStruct((t, width), BF16), jax.ShapeDtypeStruct((t, width), BF16)],
        scratch_shapes=[
            pltpu.VMEM((n_slabs, SLAB, SLAB), F32),
            slot_bf16, slot_bf16, slot_bf16, slot_bf16, slot_bf16, slot_bf16, slot_bf16,
            pltpu.VMEM((n_slots, 8, width), F32),
            slot_f32, slot_f32,
            pltpu.VMEM((2, 2, CHUNK, width), F32),
            pltpu.VMEM((2, 2, n_slabs, SLAB, SLAB), BF16),
            pltpu.VMEM((2, n_slabs, SLAB, SLAB), F32),
            pltpu.VMEM((2, n_slabs, SLAB, SLAB), BF16), pltpu.VMEM((2, n_slabs, SLAB, SLAB), BF16),
            pltpu.VMEM((2, n_slabs, SLAB, SLAB), BF16),
            pltpu.VMEM((block + 8, width), F32),
            pltpu.VMEM((8, width), F32),
            pltpu.VMEM((block, width), F32),
            pltpu.VMEM((block, LANES), F32),
        ],
        compiler_params=pltpu.CompilerParams(
            dimension_semantics=("arbitrary",), vmem_limit_bytes=VMEM_LIMIT),
        name="mixers",
    )(p, p, p, p, p, p, vec, w2p, a2p, g2p, lvec, w_bd)


def _pad_rows(w, rows, at):
    out = jnp.zeros((rows, w.shape[1]), w.dtype)
    return out.at[at:at + w.shape[0]].set(w)


def _block_diag_pairs(wa, wx):
    def bd(w):
        h, n, _ = w.shape
        w = w.reshape(h // PAIR, PAIR, n, n)
        z = jnp.zeros_like(w[:, 0])
        top = jnp.concatenate([w[:, 0], z], axis=2)
        bot = jnp.concatenate([z, w[:, 1]], axis=2)
        return jnp.concatenate([top, bot], axis=1)
    return jnp.concatenate([bd(wa), bd(wx)], axis=2)


def kernel(x, ffn1_norm, ffn1_w_gate, ffn1_w_up, ffn1_w_down, mix_norm, w_in, rwkv_mu, rwkv_w0, rwkv_w2, rwkv_a0, rwkv_a2, rwkv_g2, rwkv_k_k, rwkv_k_a, rwkv_r_k, rwkv_ln_w, rwkv_ln_b, lru_conv_w, lru_conv_b, lru_wa, lru_ba, lru_wx, lru_bx, lru_lam, lru_norm, w_out, ffn2_norm, ffn2_w_gate, ffn2_w_up, ffn2_w_down, final_norm):
    bsz, seq, d = x.shape
    depth = w_in.shape[0]
    wr = rwkv_w0.shape[1]
    wl = lru_lam.shape[1]
    n_w, n_a, n_g = rwkv_w2.shape[1], rwkv_a2.shape[1], rwkv_g2.shape[1]
    assert wr == wl and wr % SLAB == 0 and n_w + n_a == LANES and lru_wa.shape[2] == HEAD_DIM
    small_w = 4 * LANES
    g_pad = 2 * LANES
    assert n_g <= g_pad and (3 * wr) % small_w == 0 and seq % (2 * CHUNK) == 0

    row = lambda v: v.reshape(1, -1).astype(F32)
    h = x.reshape(bsz * seq, d)
    for l in range(depth):
        h = _ffn(h, row(ffn1_norm[l]), ffn1_w_gate[l].astype(BF16), ffn1_w_up[l].astype(BF16),
                 ffn1_w_down[l].astype(BF16), row(final_norm), final_norm=False)

        o_r, o_w = 0, wr
        o_k = o_w + n_w
        o_v = o_k + wr
        o_a = o_v + wr
        o_g = o_a + n_a
        o_l = o_g + n_g
        w = w_in[l]
        zpad = jnp.zeros((d, small_w - n_w - n_a - n_g), w.dtype)
        w_re = jnp.concatenate([w[:, o_r:o_r + wr], w[:, o_k:o_k + wr], w[:, o_v:o_v + wr],
                                w[:, o_l:o_l + 2 * wl],
                                w[:, o_w:o_w + n_w], w[:, o_a:o_a + n_a], w[:, o_g:o_g + n_g], zpad],
                               axis=1).astype(BF16)
        mu = rwkv_mu[l]
        mu_re = jnp.concatenate([mu[o_r:o_r + wr], mu[o_k:o_k + wr], mu[o_v:o_v + wr],
                                 jnp.zeros((2 * wl,), F32),
                                 mu[o_w:o_w + n_w], mu[o_a:o_a + n_a], mu[o_g:o_g + n_g],
                                 jnp.zeros((small_w - n_w - n_a - n_g,), F32)]).reshape(1, -1).astype(F32)
        vec = jnp.stack([rwkv_w0[l], rwkv_a0[l], rwkv_k_k[l], rwkv_k_a[l], rwkv_r_k[l].reshape(-1),
                         rwkv_ln_w[l], rwkv_ln_b[l], jnp.zeros((wr,), F32)]).astype(F32)
        w2p = _pad_rows(rwkv_w2[l], LANES, 0).astype(BF16)
        a2p = _pad_rows(rwkv_a2[l], LANES, n_w).astype(BF16)
        g2p = _pad_rows(rwkv_g2[l], g_pad, 0).astype(BF16)

        lvec = jnp.concatenate([lru_conv_w[l], jnp.stack([lru_conv_b[l], lru_ba[l], lru_bx[l],
                                                          lru_lam[l], lru_norm[l]]),
                                jnp.zeros((16 - CONV_WIDTH - 5, wl), F32)], axis=0).astype(F32)
        w_bd = _block_diag_pairs(lru_wa[l], lru_wx[l]).astype(BF16)

        p = _proj(h, row(mix_norm[l]), w_re, mu_re, rows_per_seq=seq)
        y_r, y_l = _mixers(p, vec, w2p, a2p, g2p, lvec, w_bd, width=wr, lru_x_col=3, lru_gate_col=4,
                           small_col=(3 * wr + 2 * wl) // small_w, small_w=small_w,
                           chunks_per_seq=seq // CHUNK)
        h = _outproj(h, y_r, y_l, w_out[l].astype(BF16))
        h = _ffn(h, row(ffn2_norm[l]), ffn2_w_gate[l].astype(BF16), ffn2_w_up[l].astype(BF16),
                 ffn2_w_down[l].astype(BF16), row(final_norm), final_norm=(l == depth - 1))
    return h.reshape(bsz, seq, d)
```

```python
import functools

import jax
import jax.numpy as jnp
from jax import lax
from jax.experimental import pallas as pl
from jax.experimental.pallas import tpu as pltpu

F32 = jnp.float32
BF16 = jnp.bfloat16

HEAD_DIM = 64
CONV_WIDTH = 4
LRU_C = 8.0
NORM_EPS = 1e-6
DECAY_SCALE = 0.6065306597126334
GN_EPS = 64e-5

LANES = 128
CHUNK = 64
PAIR = LANES // HEAD_DIM
SLAB = 256
VMEM_PHYSICAL_V7X = 64 * 1024 * 1024
VMEM_LIMIT = VMEM_PHYSICAL_V7X - 4 * 1024 * 1024


def _dot(a, b):
    return jnp.dot(a, b, preferred_element_type=F32)


def _dot_nt(a, b):
    return lax.dot_general(a, b, (((1,), (1,)), ((), ())), preferred_element_type=F32)


def _dot_tn(a, b):
    return lax.dot_general(a, b, (((0,), (0,)), ((), ())), preferred_element_type=F32)


def _sigmoid(x):
    return 0.5 * (jnp.tanh(0.5 * x) + 1.0)


def _softplus(x):
    return jnp.maximum(x, 0.0) + jnp.log(1.0 + jnp.exp(-jnp.abs(x)))


def _rms_norm(x, g):
    ms = jnp.mean(x * x, axis=-1, keepdims=True)
    return x * lax.rsqrt(ms + NORM_EPS) * g


def _ffn_kernel(h_ref, g_ref, wg_ref, wu_ref, wd_ref, fg_ref, o_ref, xn_ref, *, final_norm):
    j = pl.program_id(1)

    @pl.when(j == 0)
    def _():
        xn_ref[...] = _rms_norm(h_ref[...], g_ref[...]).astype(BF16)
        o_ref[...] = jnp.zeros_like(o_ref)

    xn = xn_ref[...]
    gate = _dot(xn, wg_ref[...])
    up = _dot(xn, wu_ref[...])
    act = (gate * _sigmoid(gate) * up).astype(BF16)
    o_ref[...] += _dot(act, wd_ref[...])

    @pl.when(j == pl.num_programs(1) - 1)
    def _():
        out = h_ref[...] + 0.5 * o_ref[...]
        if final_norm:
            out = _rms_norm(out, fg_ref[...])
        o_ref[...] = out


def _ffn(h, g, wg, wu, wd, fg, *, final_norm, tf=512):
    t, d = h.shape
    tm = 512 if final_norm else 1024
    dff = wg.shape[1]
    grid = (t // tm, dff // tf)
    return pl.pallas_call(
        functools.partial(_ffn_kernel, final_norm=final_norm),
        grid=grid,
        in_specs=[
            pl.BlockSpec((tm, d), lambda i, j: (i, 0)),
            pl.BlockSpec((1, d), lambda i, j: (0, 0)),
            pl.BlockSpec((d, tf), lambda i, j: (0, j)),
            pl.BlockSpec((d, tf), lambda i, j: (0, j)),
            pl.BlockSpec((tf, d), lambda i, j: (j, 0)),
            pl.BlockSpec((1, d), lambda i, j: (0, 0)),
        ],
        out_specs=pl.BlockSpec((tm, d), lambda i, j: (i, 0)),
        out_shape=jax.ShapeDtypeStruct((t, d), F32),
        scratch_shapes=[pltpu.VMEM((tm, d), BF16)],
        compiler_params=pltpu.CompilerParams(
            dimension_semantics=("parallel", "arbitrary"), vmem_limit_bytes=VMEM_LIMIT),
        name="ffn",
    )(h, g, wg, wu, wd, fg)


def _proj_kernel(h_ref, g_ref, w_ref, mu_ref, o_ref, xn_ref, last_ref, *, tiles_per_seq):
    i, j = pl.program_id(0), pl.program_id(1)

    @pl.when(j == 0)
    def _():
        xn_ref[...] = _rms_norm(h_ref[...], g_ref[...]).astype(BF16)

    @pl.when(i == 0)
    def _():
        last_ref[j] = jnp.zeros(last_ref.shape[1:], F32)

    z = _dot(xn_ref[...], w_ref[...])
    above = jnp.where(i % tiles_per_seq == 0, 0.0, last_ref[j, 0:1, :])
    last_ref[j, 0:1, :] = z[z.shape[0] - 1:, :]
    o_ref[...] = (z + (_shift_rows(z, above) - z) * mu_ref[...]).astype(o_ref.dtype)


def _proj(h, g, w, mu, *, rows_per_seq, tm=512, tn=2816):
    t, d = h.shape
    n = w.shape[1]
    assert rows_per_seq % tm == 0
    return pl.pallas_call(
        functools.partial(_proj_kernel, tiles_per_seq=rows_per_seq // tm),
        grid=(t // tm, n // tn),
        in_specs=[
            pl.BlockSpec((tm, d), lambda i, j: (i, 0)),
            pl.BlockSpec((1, d), lambda i, j: (0, 0)),
            pl.BlockSpec((d, tn), lambda i, j: (0, j)),
            pl.BlockSpec((1, tn), lambda i, j: (0, j)),
        ],
        out_specs=pl.BlockSpec((tm, tn), lambda i, j: (i, j)),
        out_shape=jax.ShapeDtypeStruct((t, n), BF16),
        scratch_shapes=[pltpu.VMEM((tm, d), BF16),
                        pltpu.VMEM((n // tn, 8, tn), F32)],
        compiler_params=pltpu.CompilerParams(
            dimension_semantics=("arbitrary", "arbitrary"), vmem_limit_bytes=VMEM_LIMIT),
        name="proj",
    )(h, g, w, mu)


def _outproj_kernel(h_ref, ya_ref, yb_ref, w_ref, o_ref):
    ka = ya_ref.shape[1]
    o_ref[...] = (h_ref[...] + _dot(ya_ref[...], w_ref[:ka, :]) + _dot(yb_ref[...], w_ref[ka:, :]))


def _outproj(h, ya, yb, w, *, tm=512):
    t, d = h.shape
    ka, kb = ya.shape[1], yb.shape[1]
    return pl.pallas_call(
        _outproj_kernel,
        grid=(t // tm,),
        in_specs=[
            pl.BlockSpec((tm, d), lambda i: (i, 0)),
            pl.BlockSpec((tm, ka), lambda i: (i, 0)),
            pl.BlockSpec((tm, kb), lambda i: (i, 0)),
            pl.BlockSpec((ka + kb, d), lambda i: (0, 0)),
        ],
        out_specs=pl.BlockSpec((tm, d), lambda i: (i, 0)),
        out_shape=jax.ShapeDtypeStruct((t, d), F32),
        compiler_params=pltpu.CompilerParams(
            dimension_semantics=("parallel",), vmem_limit_bytes=VMEM_LIMIT),
        name="outproj",
    )(h, ya, yb, w)


def _head_sum(x, ones_bd, exact):
    hi = x.astype(BF16)
    if not exact:
        return _dot(hi, ones_bd)
    r = x.shape[0]
    lo = (x - hi.astype(F32)).astype(BF16)
    z = _dot(jnp.concatenate([hi, lo], axis=0), ones_bd)
    return z[:r] + z[r:]


def _shift_rows(x, carry_row):
    rolled = pltpu.roll(x, 1, 0)
    row = lax.broadcasted_iota(jnp.int32, x.shape, 0)
    return jnp.where(row == 0, carry_row, rolled)


def _mixer_kernel(x0_ref, s0_ref, xn_ref, sn_ref, lx_ref, lg_ref,
                  vec_ref, w2_ref, a2_ref, g2_ref, lvec_ref, wbd_ref,
                  o_ref, ol_ref,
                  hs_ref, ab_ref, bt_ref, kt_ref, rb_ref, vv_ref, bh_ref, kh_ref,
                  pc_ref, bon_ref, gg_ref, y_ref, pw_ref, tb_ref, pi_ref, aak_ref, arb_ref, ark_ref,
                  ext_ref, hc_ref, yl_ref, ssq_ref,
                  *, chunks_per_seq, n_steps):
    j = pl.program_id(0)
    width = o_ref.shape[1]
    n_slabs = width // SLAB
    heads_per_slab = SLAB // HEAD_DIM

    vec = vec_ref[...]
    w0, a0, k_k, k_a, r_k, ln_w, ln_b = (vec[i:i + 1, :] for i in range(7))

    lane_s = lax.broadcasted_iota(jnp.int32, (SLAB, SLAB), 1)
    sub_s = lax.broadcasted_iota(jnp.int32, (SLAB, SLAB), 0)
    strict = lane_s < sub_s
    incl = lane_s <= sub_s
    eye = jnp.where(lane_s == sub_s, 1.0, 0.0).astype(BF16)
    ones_bd = jnp.where((lane_s // HEAD_DIM) == (sub_s // HEAD_DIM), 1.0, 0.0).astype(BF16)
    head_of_lane = lax.broadcasted_iota(jnp.int32, (CHUNK, SLAB), 1) // HEAD_DIM
    row_c = lax.broadcasted_iota(jnp.int32, (CHUNK, SLAB), 0)
    cols = [slice(q * SLAB, (q + 1) * SLAB) for q in range(n_slabs)]
    slabs = range(n_slabs)
    n_sq = CHUNK.bit_length() - 2
    halves = (slice(0, CHUNK), slice(CHUNK, 2 * CHUNK))

    def prep_ops(x_ref, s_ref, rows, slot):
        small = {}

        def lora_inputs():
            s1 = s_ref[rows, 0:LANES].astype(F32)
            s2 = s_ref[rows, LANES:LANES + g2_ref.shape[0]].astype(F32)
            small["tanh_w"] = jnp.tanh(s1).astype(BF16)
            small["lin_a"] = s_ref[rows, 0:LANES]
            small["sig_g"] = _sigmoid(s2).astype(BF16)

        val = {}

        def load(q):
            lo = q * SLAB
            val[q, "r"] = x_ref[rows, lo:lo + SLAB].astype(F32)
            val[q, "k"] = x_ref[rows, width + lo:width + lo + SLAB].astype(F32)
            vv_ref[slot, :, cols[q]] = x_ref[rows, 2 * width + lo:2 * width + lo + SLAB]
            val[q, "v"] = x_ref[rows, 2 * width + lo:2 * width + lo + SLAB].astype(F32)

        def lora(q):
            c = cols[q]
            z = w0[:, c] + _dot(small["tanh_w"], w2_ref[:, c])
            val[q, "a"] = _sigmoid(a0[:, c] + _dot(small["lin_a"], a2_ref[:, c]))
            gg_ref[slot, :, c] = _dot(small["sig_g"], g2_ref[:, c])
            val[q, "ld"] = (-DECAY_SCALE) * _sigmoid(z)

        def keys(q):
            c = cols[q]
            k = val[q, "k"]
            kk = k * k_k[:, c]
            val[q, "kk"] = kk * lax.rsqrt(jnp.maximum(_head_sum(kk * kk, ones_bd, exact=True), 1e-24))
            val[q, "k"] = k * (1.0 + (val[q, "a"] - 1.0) * k_a[:, c])

        def bonus(q):
            c = cols[q]
            rk = val[q, "r"] * val[q, "k"] * r_k[:, c]
            bon_ref[slot, :, c] = _head_sum(rk, ones_bd, exact=False) * val.pop((q, "v"))

        def decay(q):
            cum = val[q, "ld"]
            sh = 1
            while sh < CHUNK:
                cum = cum + jnp.where(row_c >= sh, pltpu.roll(cum, sh, 0), 0.0)
                sh *= 2
            val[q, "cum"] = cum

        def scale_in(q):
            c = cols[q]
            cum, ld, kk, r = val[q, "cum"], val.pop((q, "ld")), val[q, "kk"], val.pop((q, "r"))
            ab_ref[slot, :, c] = (-kk * jnp.exp(cum - ld)).astype(BF16)
            rb_ref[slot, :, c] = (r * jnp.exp(cum)).astype(BF16)

        def scale_out(q):
            c = cols[q]
            cum, kk, a, k = val.pop((q, "cum")), val.pop((q, "kk")), val.pop((q, "a")), val.pop((q, "k"))
            tot = cum[CHUNK - 1:CHUNK, :]
            kb = kk * a
            e_inv = jnp.exp(-cum)
            e_end = jnp.exp(tot - cum)
            bt_ref[slot, :, c] = (kb * e_inv).astype(BF16)
            kt_ref[slot, :, c] = (k * e_inv).astype(BF16)
            bh_ref[slot, :, c] = (kb * e_end).astype(BF16)
            kh_ref[slot, :, c] = (k * e_end).astype(BF16)
            pc_ref[slot, 0:1, c] = jnp.exp(tot)

        stages = (load, lora, keys, bonus, decay, scale_in, scale_out)
        return [lora_inputs] + [functools.partial(stage, q) for q in slabs for stage in stages]

    def stack(x):
        zero = jnp.zeros_like(x)
        return jnp.concatenate([jnp.where(head_of_lane == hd, x, zero) for hd in range(heads_per_slab)],
                               axis=0)

    def chain_ops(slot, half, ybuf):
        live = {}

        def gram(q):
            xx = jnp.concatenate([stack(ab_ref[slot, :, cols[q]]), stack(rb_ref[slot, :, cols[q]])], axis=0)
            gb = _dot_nt(xx, stack(bt_ref[slot, :, cols[q]]))
            gk = _dot_nt(xx, stack(kt_ref[slot, :, cols[q]]))
            a_ab = jnp.where(strict, gb[:SLAB], 0.0).astype(BF16)
            pw_ref[half, 0, q] = a_ab
            tb_ref[half, q] = a_ab + eye
            aak_ref[half, q] = jnp.where(strict, gk[:SLAB], 0.0).astype(BF16)
            arb_ref[half, q] = jnp.where(incl, gb[SLAB:], 0.0).astype(BF16)
            ark_ref[half, q] = jnp.where(incl, gk[SLAB:], 0.0).astype(BF16)

        def square(s, q):
            p2 = _dot(pw_ref[half, s % 2, q], pw_ref[half, s % 2, q]).astype(BF16)
            pw_ref[half, (s + 1) % 2, q] = p2
            pi_ref[half, q] = p2 + eye

        def extend(s, q):
            tb_ref[half, q] = _dot(tb_ref[half, q], pi_ref[half, q]).astype(BF16)

        def project(q):
            vs = stack(vv_ref[slot, :, cols[q]])
            xx = jnp.concatenate([stack(ab_ref[slot, :, cols[q]]), stack(rb_ref[slot, :, cols[q]])], axis=0)
            live[q] = (vs, _dot_nt(xx, hs_ref[q].astype(BF16)))

        def rhs(q):
            vs, xh = live[q]
            live[q] = (vs, xh[SLAB:], (xh[:SLAB] + _dot(aak_ref[half, q], vs)).astype(BF16))

        def solve(q):
            vs, xrh, b = live[q]
            live[q] = (vs, xrh, _dot(tb_ref[half, q], b).astype(BF16))

        def emit(q):
            vs, xrh, u = live[q]
            y_st = xrh + _dot(arb_ref[half, q], u) + _dot(ark_ref[half, q], vs)
            y = y_st[0:CHUNK]
            for hd in range(1, heads_per_slab):
                y = y + y_st[hd * CHUNK:(hd + 1) * CHUNK]
            y_ref[ybuf, half, :, cols[q]] = y

        def advance(q):
            vs, _, u = live.pop(q)
            pc = pc_ref[slot, 0:1, cols[q]]
            hs_ref[q] = (hs_ref[q] * pc + _dot_tn(u, stack(bh_ref[slot, :, cols[q]]))
                         + _dot_tn(vs, stack(kh_ref[slot, :, cols[q]])))

        free = [gram]
        for s in range(n_sq):
            free += [functools.partial(square, s), functools.partial(extend, s)]
        bound = [project, rhs, solve, emit, advance]
        per_slab = lambda stages: [[functools.partial(stage, q) for q in slabs] for stage in stages]
        return per_slab(free), per_slab(bound)

    def post_ops(slot, half, ybuf):
        val = {}
        inv_n = 1.0 / HEAD_DIM

        def center(q):
            y = y_ref[ybuf, half, :, cols[q]]
            val[q] = y - _head_sum(y, ones_bd, exact=False) * inv_n

        def scale(q):
            c = cols[q]
            yc = val.pop(q)
            var = _head_sum(yc * yc, ones_bd, exact=False) * inv_n
            yn = yc * lax.rsqrt(var + GN_EPS) * ln_w[:, c] + ln_b[:, c]
            o_ref[halves[half], c] = ((yn + bon_ref[slot, :, c]) * gg_ref[slot, :, c]).astype(o_ref.dtype)

        return [functools.partial(stage, q) for q in slabs for stage in (center, scale)]

    def lru_ops():
        n_rows = 2 * CHUNK
        pad = 8
        lvec = lvec_ref[...]
        conv_w = [lvec[i:i + 1, :] for i in range(CONV_WIDTH)]
        conv_b, b_a, b_x, lam, norm_g = (lvec[i:i + 1, :] for i in range(CONV_WIDTH, CONV_WIDTH + 5))
        row8 = lax.broadcasted_iota(jnp.int32, (8, LANES), 0)
        lane_cols = [slice(s * LANES, (s + 1) * LANES) for s in range(width // LANES)]
        val = {}

        def gates(s):
            ln = lane_cols[s]
            x = lx_ref[:, ln].astype(F32)
            ext_ref[pad:pad + n_rows, ln] = x
            xc = conv_b[:, ln] + conv_w[CONV_WIDTH - 1][:, ln] * x
            for i in range(CONV_WIDTH - 1):
                xc = xc + conv_w[i][:, ln] * ext_ref[pl.ds(pad - (CONV_WIDTH - 1) + i, n_rows), ln]
            ext_ref[0:pad, ln] = ext_ref[n_rows:n_rows + pad, ln]
            z = _dot(xc.astype(BF16), wbd_ref[s])
            r = _sigmoid(z[:, :LANES] + b_a[:, ln])
            i_gate = _sigmoid(z[:, LANES:] + b_x[:, ln])
            log_a = (-LRU_C) * r * _softplus(-lam[:, ln])
            a = jnp.exp(log_a)
            val[s] = (a, jnp.sqrt(-jnp.tanh(log_a) * (a * a + 1.0)) * (i_gate * xc))

        n_groups = n_rows // 8
        scan_parts = 4

        def scan(s, part):
            ln = lane_cols[s]
            a, u = val[s]
            if part == 0:
                val[s, "hc"] = hc_ref[:, ln]
                val[s, "h"] = []
            hc = val[s, "hc"]
            for g in range(part * n_groups // scan_parts, (part + 1) * n_groups // scan_parts):
                aa, uu = a[8 * g:8 * g + 8], u[8 * g:8 * g + 8]
                for sh in (1, 2, 4):
                    ok = row8 >= sh
                    uu = jnp.where(ok, aa * pltpu.roll(uu, sh, 0) + uu, uu)
                    aa = jnp.where(ok, aa * pltpu.roll(aa, sh, 0), aa)
                hh = aa * hc + uu
                val[s, "h"].append(hh)
                hc = jnp.broadcast_to(hh[7:8, :], (8, LANES))
            val[s, "hc"] = hc
            if part == scan_parts - 1:
                hc_ref[:, ln] = val.pop((s, "hc"))
                del val[s]

        def gate(s):
            ln = lane_cols[s]
            gt = lg_ref[:, ln].astype(F32)
            gelu = 0.5 * gt * (1.0 + jnp.tanh(0.7978845608028654 * (gt + 0.044715 * gt * gt * gt)))
            y = jnp.concatenate(val.pop((s, "h")), axis=0) * gelu
            yl_ref[:, ln] = y
            ssq_ref[...] = y * y if s == 0 else ssq_ref[...] + y * y

        def finish():
            ms = jnp.sum(ssq_ref[...], axis=-1, keepdims=True) * (1.0 / width)
            scale = lax.rsqrt(ms + NORM_EPS)
            for ln in lane_cols:
                ol_ref[:, ln] = (yl_ref[:, ln] * scale * norm_g[:, ln]).astype(ol_ref.dtype)

        ops = []
        for s in range(len(lane_cols)):
            ops += [functools.partial(gates, s)]
            ops += [functools.partial(scan, s, part) for part in range(scan_parts)]
            ops += [functools.partial(gate, s)]
        return ops + [finish]

    def interleave(main, fill):
        n, m = len(main), len(fill)
        done = 0
        for i, op in enumerate(main):
            op()
            while done < m and (done + 1) * n <= (i + 1) * (m + 1):
                fill[done]()
                done += 1
        for op in fill[done:]:
            op()

    slot_sets = ((0, 1), (2, 3))

    def step(parity):
        chain_slots, prep_slots = slot_sets[parity], slot_sets[1 - parity]
        free0, bound0 = chain_ops(chain_slots[0], 0, parity)
        free1, bound1 = chain_ops(chain_slots[1], 1, parity)
        main = [op for pair in zip(free0, free1) for stage in pair for op in stage]
        tail = [op for stage in bound0 + bound1 for op in stage]
        lru = lru_ops()
        fill = []
        for half in range(2):
            fill += post_ops(prep_slots[half], half, 1 - parity)
            fill += prep_ops(xn_ref, sn_ref, halves[half], prep_slots[half])
            fill += lru[half * len(lru) // 2:(half + 1) * len(lru) // 2]
        cut = len(fill) // 2
        interleave(main, fill[:cut])
        interleave(tail, fill[cut:])

    @pl.when(j == 0)
    def _():
        y_ref[1] = jnp.zeros_like(y_ref[1])
        for slot in slot_sets[1]:
            bon_ref[slot] = jnp.zeros_like(bon_ref[slot])
            gg_ref[slot] = jnp.zeros_like(gg_ref[slot])
        for op in prep_ops(x0_ref, s0_ref, halves[0], 0) + prep_ops(x0_ref, s0_ref, halves[1], 1):
            op()

    @pl.when((2 * j) % chunks_per_seq == 0)
    def _():
        hs_ref[...] = jnp.zeros_like(hs_ref)
        hc_ref[...] = jnp.zeros_like(hc_ref)
        ext_ref[0:8, :] = jnp.zeros((8, width), F32)

    for parity in range(2):
        @pl.when((j % 2 == parity) & (j < n_steps))
        def _():
            step(parity)

    @pl.when(j == n_steps)
    def _():
        last = n_steps % 2
        for half in range(2):
            for op in post_ops(slot_sets[1 - last][half], half, 1 - last):
                op()


def _mixers(p, vec, w2p, a2p, g2p, lvec, w_bd, *, width, lru_x_col, lru_gate_col, small_col, small_w,
            chunks_per_seq):
    t = p.shape[0]
    n_chunks = t // CHUNK
    assert chunks_per_seq % 2 == 0 and n_chunks % chunks_per_seq == 0
    n_slabs = width // SLAB
    n_steps = n_chunks // 2
    block = 2 * CHUNK
    views = []
    for idx in (lambda j: 0, lambda j: jnp.minimum(j + 1, n_steps - 1)):
        views.append(pl.BlockSpec((block, 3 * width), lambda j, idx=idx: (idx(j), 0)))
        views.append(pl.BlockSpec((block, small_w), lambda j, idx=idx: (idx(j), small_col)))
    current = lambda j: jnp.minimum(j, n_steps - 1)
    views.append(pl.BlockSpec((block, width), lambda j: (current(j), lru_x_col)))
    views.append(pl.BlockSpec((block, width), lambda j: (current(j), lru_gate_col)))
    full = lambda a: pl.BlockSpec(a.shape, lambda j: (0,) * a.ndim)
    n_slots = 4
    slot_bf16 = pltpu.VMEM((n_slots, CHUNK, width), BF16)
    slot_f32 = pltpu.VMEM((n_slots, CHUNK, width), F32)
    return pl.pallas_call(
        functools.partial(_mixer_kernel, chunks_per_seq=chunks_per_seq, n_steps=n_steps),
        grid=(n_steps + 1,),
        in_specs=views + [full(vec), full(w2p), full(a2p), full(g2p), full(lvec), full(w_bd)],
        out_specs=[pl.BlockSpec((block, width), lambda j: (jnp.maximum(j - 1, 0), 0)),
                   pl.BlockSpec((block, width), lambda j: (current(j), 0))],
        out_shape=[jax.ShapeDtypeStruct((t, width), BF16), jax.ShapeDtypeStruct((t, width), BF16)],
        scratch_shapes=[
            pltpu.VMEM((n_slabs, SLAB, SLAB), F32),
            slot_bf16, slot_bf16, slot_bf16, slot_bf16, slot_bf16, slot_bf16, slot_bf16,
            pltpu.VMEM((n_slots, 8, width), F32),
            slot_f32, slot_f32,
            pltpu.VMEM((2, 2, CHUNK, width), F32),
            pltpu.VMEM((2, 2, n_slabs, SLAB, SLAB), BF16),
            pltpu.VMEM((2, n_slabs, SLAB, SLAB), BF16),
            pltpu.VMEM((2, n_slabs, SLAB, SLAB), BF16),
            pltpu.VMEM((2, n_slabs, SLAB, SLAB), BF16), pltpu.VMEM((2, n_slabs, SLAB, SLAB), BF16),
            pltpu.VMEM((2, n_slabs, SLAB, SLAB), BF16),
            pltpu.VMEM((block + 8, width), F32),
            pltpu.VMEM((8, width), F32),
            pltpu.VMEM((block, width), F32),
            pltpu.VMEM((block, LANES), F32),
        ],
        compiler_params=pltpu.CompilerParams(
            dimension_semantics=("arbitrary",), vmem_limit_bytes=VMEM_LIMIT),
        name="mixers",
    )(p, p, p, p, p, p, vec, w2p, a2p, g2p, lvec, w_bd)


def _pad_rows(w, rows, at):
    out = jnp.zeros((rows, w.shape[1]), w.dtype)
    return out.at[at:at + w.shape[0]].set(w)


def _block_diag_pairs(wa, wx):
    def bd(w):
        h, n, _ = w.shape
        w = w.reshape(h // PAIR, PAIR, n, n)
        z = jnp.zeros_like(w[:, 0])
        top = jnp.concatenate([w[:, 0], z], axis=2)
        bot = jnp.concatenate([z, w[:, 1]], axis=2)
        return jnp.concatenate([top, bot], axis=1)
    return jnp.concatenate([bd(wa), bd(wx)], axis=2)


def kernel(x, ffn1_norm, ffn1_w_gate, ffn1_w_up, ffn1_w_down, mix_norm, w_in, rwkv_mu, rwkv_w0, rwkv_w2, rwkv_a0, rwkv_a2, rwkv_g2, rwkv_k_k, rwkv_k_a, rwkv_r_k, rwkv_ln_w, rwkv_ln_b, lru_conv_w, lru_conv_b, lru_wa, lru_ba, lru_wx, lru_bx, lru_lam, lru_norm, w_out, ffn2_norm, ffn2_w_gate, ffn2_w_up, ffn2_w_down, final_norm):
    bsz, seq, d = x.shape
    depth = w_in.shape[0]
    wr = rwkv_w0.shape[1]
    wl = lru_lam.shape[1]
    n_w, n_a, n_g = rwkv_w2.shape[1], rwkv_a2.shape[1], rwkv_g2.shape[1]
    assert wr == wl and wr % SLAB == 0 and n_w + n_a == LANES and lru_wa.shape[2] == HEAD_DIM
    small_w = 4 * LANES
    g_pad = 2 * LANES
    assert n_g <= g_pad and (3 * wr) % small_w == 0 and seq % (2 * CHUNK) == 0

    row = lambda v: v.reshape(1, -1).astype(F32)
    h = x.reshape(bsz * seq, d)
    for l in range(depth):
        h = _ffn(h, row(ffn1_norm[l]), ffn1_w_gate[l].astype(BF16), ffn1_w_up[l].astype(BF16),
                 ffn1_w_down[l].astype(BF16), row(final_norm), final_norm=False)

        o_r, o_w = 0, wr
        o_k = o_w + n_w
        o_v = o_k + wr
        o_a = o_v + wr
        o_g = o_a + n_a
        o_l = o_g + n_g
        w = w_in[l]
        zpad = jnp.zeros((d, small_w - n_w - n_a - n_g), BF16)
        w_re = jnp.concatenate([w[:, o_r:o_r + wr].astype(BF16), w[:, o_k:o_k + wr].astype(BF16),
                                w[:, o_v:o_v + wr].astype(BF16), w[:, o_l:o_l + 2 * wl].astype(BF16),
                                w[:, o_w:o_w + n_w].astype(BF16), w[:, o_a:o_a + n_a].astype(BF16),
                                w[:, o_g:o_g + n_g].astype(BF16), zpad], axis=1)
        mu = rwkv_mu[l]
        mu_re = jnp.concatenate([mu[o_r:o_r + wr], mu[o_k:o_k + wr], mu[o_v:o_v + wr],
                                 jnp.zeros((2 * wl,), F32),
                                 mu[o_w:o_w + n_w], mu[o_a:o_a + n_a], mu[o_g:o_g + n_g],
                                 jnp.zeros((small_w - n_w - n_a - n_g,), F32)]).reshape(1, -1).astype(F32)
        vec = jnp.stack([rwkv_w0[l], rwkv_a0[l], rwkv_k_k[l], rwkv_k_a[l], rwkv_r_k[l].reshape(-1),
                         rwkv_ln_w[l], rwkv_ln_b[l], jnp.zeros((wr,), F32)]).astype(F32)
        w2p = _pad_rows(rwkv_w2[l], LANES, 0).astype(BF16)
        a2p = _pad_rows(rwkv_a2[l], LANES, n_w).astype(BF16)
        g2p = _pad_rows(rwkv_g2[l], g_pad, 0).astype(BF16)

        lvec = jnp.concatenate([lru_conv_w[l], jnp.stack([lru_conv_b[l], lru_ba[l], lru_bx[l],
                                                          lru_lam[l], lru_norm[l]]),
                                jnp.zeros((16 - CONV_WIDTH - 5, wl), F32)], axis=0).astype(F32)
        w_bd = _block_diag_pairs(lru_wa[l], lru_wx[l]).astype(BF16)

        p = _proj(h, row(mix_norm[l]), w_re, mu_re, rows_per_seq=seq)
        y_r, y_l = _mixers(p, vec, w2p, a2p, g2p, lvec, w_bd, width=wr, lru_x_col=3, lru_gate_col=4,
                           small_col=(3 * wr + 2 * wl) // small_w, small_w=small_w,
                           chunks_per_seq=seq // CHUNK)
        h = _outproj(h, y_r, y_l, w_out[l].astype(BF16))
        h = _ffn(h, row(ffn2_norm[l]), ffn2_w_gate[l].astype(BF16), ffn2_w_up[l].astype(BF16),
                 ffn2_w_down[l].astype(BF16), row(final_norm), final_norm=(l == depth - 1))
    return h.reshape(bsz, seq, d)
```

```python
import functools

import jax
import jax.numpy as jnp
from jax import lax
from jax.experimental import pallas as pl
from jax.experimental.pallas import tpu as pltpu

F32 = jnp.float32
BF16 = jnp.bfloat16

HEAD_DIM = 64
CONV_WIDTH = 4
LRU_C = 8.0
NORM_EPS = 1e-6
DECAY_SCALE = 0.6065306597126334
GN_EPS = 64e-5

LANES = 128
CHUNK = 64
PAIR = LANES // HEAD_DIM
SLAB = 256
VMEM_PHYSICAL_V7X = 64 * 1024 * 1024
VMEM_LIMIT = VMEM_PHYSICAL_V7X - 4 * 1024 * 1024
VMEM_LIMIT_FFN = VMEM_PHYSICAL_V7X - 1024 * 1024


def _dot(a, b):
    return jnp.dot(a, b, preferred_element_type=F32)


def _dot_nt(a, b):
    return lax.dot_general(a, b, (((1,), (1,)), ((), ())), preferred_element_type=F32)


def _dot_tn(a, b):
    return lax.dot_general(a, b, (((0,), (0,)), ((), ())), preferred_element_type=F32)


def _sigmoid(x):
    return 0.5 * (jnp.tanh(0.5 * x) + 1.0)


def _softplus(x):
    return jnp.maximum(x, 0.0) + jnp.log(1.0 + jnp.exp(-jnp.abs(x)))


def _rms_norm(x, g):
    ms = jnp.mean(x * x, axis=-1, keepdims=True)
    return x * lax.rsqrt(ms + NORM_EPS) * g


def _ffn_kernel(h_ref, g_ref, wg_ref, wu_ref, wd_ref, fg_ref, o_ref, xn_ref, *, final_norm):
    j = pl.program_id(1)

    @pl.when(j == 0)
    def _():
        xn_ref[...] = _rms_norm(h_ref[...], g_ref[...]).astype(BF16)
        o_ref[...] = jnp.zeros_like(o_ref)

    xn = xn_ref[...]
    gate = _dot(xn, wg_ref[...])
    up = _dot(xn, wu_ref[...])
    act = (gate * _sigmoid(gate) * up).astype(BF16)
    o_ref[...] += _dot(act, wd_ref[...])

    @pl.when(j == pl.num_programs(1) - 1)
    def _():
        out = h_ref[...] + 0.5 * o_ref[...]
        if final_norm:
            out = _rms_norm(out, fg_ref[...])
        o_ref[...] = out


def _ffn(h, g, wg, wu, wd, fg, *, final_norm, tm=1024, tf=512):
    t, d = h.shape
    dff = wg.shape[1]
    grid = (t // tm, dff // tf)
    return pl.pallas_call(
        functools.partial(_ffn_kernel, final_norm=final_norm),
        grid=grid,
        in_specs=[
            pl.BlockSpec((tm, d), lambda i, j: (i, 0)),
            pl.BlockSpec((1, d), lambda i, j: (0, 0)),
            pl.BlockSpec((d, tf), lambda i, j: (0, j)),
            pl.BlockSpec((d, tf), lambda i, j: (0, j)),
            pl.BlockSpec((tf, d), lambda i, j: (j, 0)),
            pl.BlockSpec((1, d), lambda i, j: (0, 0)),
        ],
        out_specs=pl.BlockSpec((tm, d), lambda i, j: (i, 0)),
        out_shape=jax.ShapeDtypeStruct((t, d), F32),
        scratch_shapes=[pltpu.VMEM((tm, d), BF16)],
        compiler_params=pltpu.CompilerParams(
            dimension_semantics=("parallel", "arbitrary"), vmem_limit_bytes=VMEM_LIMIT_FFN),
        name="ffn",
    )(h, g, wg, wu, wd, fg)


def _proj_kernel(h_ref, g_ref, w_ref, mu_ref, o_ref, xn_ref, last_ref, *, tiles_per_seq):
    i, j = pl.program_id(0), pl.program_id(1)

    @pl.when(j == 0)
    def _():
        xn_ref[...] = _rms_norm(h_ref[...], g_ref[...]).astype(BF16)

    @pl.when(i == 0)
    def _():
        last_ref[j] = jnp.zeros(last_ref.shape[1:], F32)

    z = _dot(xn_ref[...], w_ref[...])
    above = jnp.where(i % tiles_per_seq == 0, 0.0, last_ref[j, 0:1, :])
    last_ref[j, 0:1, :] = z[z.shape[0] - 1:, :]
    o_ref[...] = (z + (_shift_rows(z, above) - z) * mu_ref[...]).astype(o_ref.dtype)


def _proj(h, g, w, mu, *, rows_per_seq, tm=512, tn=2816):
    t, d = h.shape
    n = w.shape[1]
    assert rows_per_seq % tm == 0
    return pl.pallas_call(
        functools.partial(_proj_kernel, tiles_per_seq=rows_per_seq // tm),
        grid=(t // tm, n // tn),
        in_specs=[
            pl.BlockSpec((tm, d), lambda i, j: (i, 0)),
            pl.BlockSpec((1, d), lambda i, j: (0, 0)),
            pl.BlockSpec((d, tn), lambda i, j: (0, j)),
            pl.BlockSpec((1, tn), lambda i, j: (0, j)),
        ],
        out_specs=pl.BlockSpec((tm, tn), lambda i, j: (i, j)),
        out_shape=jax.ShapeDtypeStruct((t, n), BF16),
        scratch_shapes=[pltpu.VMEM((tm, d), BF16),
                        pltpu.VMEM((n // tn, 8, tn), F32)],
        compiler_params=pltpu.CompilerParams(
            dimension_semantics=("arbitrary", "arbitrary"), vmem_limit_bytes=VMEM_LIMIT),
        name="proj",
    )(h, g, w, mu)


def _outproj_kernel(h_ref, ya_ref, yb_ref, w_ref, o_ref):
    ka = ya_ref.shape[1]
    o_ref[...] = (h_ref[...] + _dot(ya_ref[...], w_ref[:ka, :]) + _dot(yb_ref[...], w_ref[ka:, :]))


def _outproj(h, ya, yb, w, *, tm=512):
    t, d = h.shape
    ka, kb = ya.shape[1], yb.shape[1]
    return pl.pallas_call(
        _outproj_kernel,
        grid=(t // tm,),
        in_specs=[
            pl.BlockSpec((tm, d), lambda i: (i, 0)),
            pl.BlockSpec((tm, ka), lambda i: (i, 0)),
            pl.BlockSpec((tm, kb), lambda i: (i, 0)),
            pl.BlockSpec((ka + kb, d), lambda i: (0, 0)),
        ],
        out_specs=pl.BlockSpec((tm, d), lambda i: (i, 0)),
        out_shape=jax.ShapeDtypeStruct((t, d), F32),
        compiler_params=pltpu.CompilerParams(
            dimension_semantics=("parallel",), vmem_limit_bytes=VMEM_LIMIT),
        name="outproj",
    )(h, ya, yb, w)


def _head_sum(x, ones_bd, exact):
    hi = x.astype(BF16)
    if not exact:
        return _dot(hi, ones_bd)
    r = x.shape[0]
    lo = (x - hi.astype(F32)).astype(BF16)
    z = _dot(jnp.concatenate([hi, lo], axis=0), ones_bd)
    return z[:r] + z[r:]


def _shift_rows(x, carry_row):
    rolled = pltpu.roll(x, 1, 0)
    row = lax.broadcasted_iota(jnp.int32, x.shape, 0)
    return jnp.where(row == 0, carry_row, rolled)


def _mixer_kernel(x0_ref, s0_ref, xn_ref, sn_ref, lx_ref, lg_ref,
                  vec_ref, w2_ref, a2_ref, g2_ref, lvec_ref, wbd_ref,
                  o_ref, ol_ref,
                  hs_ref, ab_ref, bt_ref, kt_ref, rb_ref, vv_ref, bh_ref, kh_ref,
                  pc_ref, bon_ref, gg_ref, y_ref, pw_ref, tb_ref, pi_ref, aak_ref, arb_ref, ark_ref,
                  ext_ref, hc_ref, yl_ref, ssq_ref,
                  *, chunks_per_seq, n_steps):
    j = pl.program_id(0)
    width = o_ref.shape[1]
    n_slabs = width // SLAB
    heads_per_slab = SLAB // HEAD_DIM

    vec = vec_ref[...]
    w0, a0, k_k, k_a, r_k, ln_w, ln_b = (vec[i:i + 1, :] for i in range(7))

    lane_s = lax.broadcasted_iota(jnp.int32, (SLAB, SLAB), 1)
    sub_s = lax.broadcasted_iota(jnp.int32, (SLAB, SLAB), 0)
    strict = lane_s < sub_s
    incl = lane_s <= sub_s
    eye = jnp.where(lane_s == sub_s, 1.0, 0.0).astype(BF16)
    ones_bd = jnp.where((lane_s // HEAD_DIM) == (sub_s // HEAD_DIM), 1.0, 0.0).astype(BF16)
    head_of_lane = lax.broadcasted_iota(jnp.int32, (CHUNK, SLAB), 1) // HEAD_DIM
    row_c = lax.broadcasted_iota(jnp.int32, (CHUNK, SLAB), 0)
    cols = [slice(q * SLAB, (q + 1) * SLAB) for q in range(n_slabs)]
    slabs = range(n_slabs)
    n_sq = CHUNK.bit_length() - 2
    halves = (slice(0, CHUNK), slice(CHUNK, 2 * CHUNK))

    def prep_ops(x_ref, s_ref, rows, slot):
        small = {}

        def lora_inputs():
            s1 = s_ref[rows, 0:LANES].astype(F32)
            s2 = s_ref[rows, LANES:LANES + g2_ref.shape[0]].astype(F32)
            small["tanh_w"] = jnp.tanh(s1).astype(BF16)
            small["lin_a"] = s_ref[rows, 0:LANES]
            small["sig_g"] = _sigmoid(s2).astype(BF16)

        val = {}

        def load(q):
            lo = q * SLAB
            val[q, "r"] = x_ref[rows, lo:lo + SLAB].astype(F32)
            val[q, "k"] = x_ref[rows, width + lo:width + lo + SLAB].astype(F32)
            vv_ref[slot, :, cols[q]] = x_ref[rows, 2 * width + lo:2 * width + lo + SLAB]
            val[q, "v"] = x_ref[rows, 2 * width + lo:2 * width + lo + SLAB].astype(F32)

        def lora(q):
            c = cols[q]
            z = w0[:, c] + _dot(small["tanh_w"], w2_ref[:, c])
            val[q, "a"] = _sigmoid(a0[:, c] + _dot(small["lin_a"], a2_ref[:, c]))
            gg_ref[slot, :, c] = _dot(small["sig_g"], g2_ref[:, c])
            val[q, "ld"] = (-DECAY_SCALE) * _sigmoid(z)

        def keys(q):
            c = cols[q]
            k = val[q, "k"]
            kk = k * k_k[:, c]
            val[q, "kk"] = kk * lax.rsqrt(jnp.maximum(_head_sum(kk * kk, ones_bd, exact=True), 1e-24))
            val[q, "k"] = k * (1.0 + (val[q, "a"] - 1.0) * k_a[:, c])

        def bonus(q):
            c = cols[q]
            rk = val[q, "r"] * val[q, "k"] * r_k[:, c]
            bon_ref[slot, :, c] = _head_sum(rk, ones_bd, exact=False) * val.pop((q, "v"))

        def decay(q):
            cum = val[q, "ld"]
            sh = 1
            while sh < CHUNK:
                cum = cum + jnp.where(row_c >= sh, pltpu.roll(cum, sh, 0), 0.0)
                sh *= 2
            val[q, "cum"] = cum

        def scale_in(q):
            c = cols[q]
            cum, ld, kk, r = val[q, "cum"], val.pop((q, "ld")), val[q, "kk"], val.pop((q, "r"))
            ab_ref[slot, :, c] = (-kk * jnp.exp(cum - ld)).astype(BF16)
            rb_ref[slot, :, c] = (r * jnp.exp(cum)).astype(BF16)

        def scale_out(q):
            c = cols[q]
            cum, kk, a, k = val.pop((q, "cum")), val.pop((q, "kk")), val.pop((q, "a")), val.pop((q, "k"))
            tot = cum[CHUNK - 1:CHUNK, :]
            kb = kk * a
            e_inv = jnp.exp(-cum)
            e_end = jnp.exp(tot - cum)
            bt_ref[slot, :, c] = (kb * e_inv).astype(BF16)
            kt_ref[slot, :, c] = (k * e_inv).astype(BF16)
            bh_ref[slot, :, c] = (kb * e_end).astype(BF16)
            kh_ref[slot, :, c] = (k * e_end).astype(BF16)
            pc_ref[slot, 0:1, c] = jnp.exp(tot)

        stages = (load, lora, keys, bonus, decay, scale_in, scale_out)
        return [lora_inputs] + [functools.partial(stage, q) for q in slabs for stage in stages]

    def stack(x):
        zero = jnp.zeros_like(x)
        return jnp.concatenate([jnp.where(head_of_lane == hd, x, zero) for hd in range(heads_per_slab)],
                               axis=0)

    def chain_ops(slot, half, ybuf):
        live = {}

        def gram(q):
            xx = jnp.concatenate([stack(ab_ref[slot, :, cols[q]]), stack(rb_ref[slot, :, cols[q]])], axis=0)
            gb = _dot_nt(xx, stack(bt_ref[slot, :, cols[q]]))
            gk = _dot_nt(xx, stack(kt_ref[slot, :, cols[q]]))
            a_ab = jnp.where(strict, gb[:SLAB], 0.0).astype(BF16)
            pw_ref[half, 0, q] = a_ab
            tb_ref[half, q] = a_ab + eye
            aak_ref[half, q] = jnp.where(strict, gk[:SLAB], 0.0).astype(BF16)
            arb_ref[half, q] = jnp.where(incl, gb[SLAB:], 0.0).astype(BF16)
            ark_ref[half, q] = jnp.where(incl, gk[SLAB:], 0.0).astype(BF16)

        def square(s, q):
            p2 = _dot(pw_ref[half, s % 2, q], pw_ref[half, s % 2, q]).astype(BF16)
            pw_ref[half, (s + 1) % 2, q] = p2
            pi_ref[half, q] = p2 + eye

        def extend(s, q):
            tb_ref[half, q] = _dot(tb_ref[half, q], pi_ref[half, q]).astype(BF16)

        def project(q):
            vs = stack(vv_ref[slot, :, cols[q]])
            xx = jnp.concatenate([stack(ab_ref[slot, :, cols[q]]), stack(rb_ref[slot, :, cols[q]])], axis=0)
            live[q] = (vs, _dot_nt(xx, hs_ref[q].astype(BF16)))

        def rhs(q):
            vs, xh = live[q]
            live[q] = (vs, xh[SLAB:], (xh[:SLAB] + _dot(aak_ref[half, q], vs)).astype(BF16))

        def solve(q):
            vs, xrh, b = live[q]
            live[q] = (vs, xrh, _dot(tb_ref[half, q], b).astype(BF16))

        def emit(q):
            vs, xrh, u = live[q]
            y_st = xrh + _dot(arb_ref[half, q], u) + _dot(ark_ref[half, q], vs)
            y = y_st[0:CHUNK]
            for hd in range(1, heads_per_slab):
                y = y + y_st[hd * CHUNK:(hd + 1) * CHUNK]
            y_ref[ybuf, half, :, cols[q]] = y

        def advance(q):
            vs, _, u = live.pop(q)
            pc = pc_ref[slot, 0:1, cols[q]]
            hs_ref[q] = (hs_ref[q] * pc + _dot_tn(u, stack(bh_ref[slot, :, cols[q]]))
                         + _dot_tn(vs, stack(kh_ref[slot, :, cols[q]])))

        free = [gram]
        for s in range(n_sq):
            free += [functools.partial(square, s), functools.partial(extend, s)]
        bound = [project, rhs, solve, emit, advance]
        per_slab = lambda stages: [[functools.partial(stage, q) for q in slabs] for stage in stages]
        return per_slab(free), per_slab(bound)

    def post_ops(slot, half, ybuf):
        val = {}
        inv_n = 1.0 / HEAD_DIM

        def center(q):
            y = y_ref[ybuf, half, :, cols[q]]
            val[q] = y - _head_sum(y, ones_bd, exact=False) * inv_n

        def scale(q):
            c = cols[q]
            yc = val.pop(q)
            var = _head_sum(yc * yc, ones_bd, exact=False) * inv_n
            yn = yc * lax.rsqrt(var + GN_EPS) * ln_w[:, c] + ln_b[:, c]
            o_ref[halves[half], c] = ((yn + bon_ref[slot, :, c]) * gg_ref[slot, :, c]).astype(o_ref.dtype)

        return [functools.partial(stage, q) for q in slabs for stage in (center, scale)]

    def lru_ops():
        n_rows = 2 * CHUNK
        pad = 8
        lvec = lvec_ref[...]
        conv_w = [lvec[i:i + 1, :] for i in range(CONV_WIDTH)]
        conv_b, b_a, b_x, lam, norm_g = (lvec[i:i + 1, :] for i in range(CONV_WIDTH, CONV_WIDTH + 5))
        row8 = lax.broadcasted_iota(jnp.int32, (8, LANES), 0)
        lane_cols = [slice(s * LANES, (s + 1) * LANES) for s in range(width // LANES)]
        val = {}

        def gates(s):
            ln = lane_cols[s]
            x = lx_ref[:, ln].astype(F32)
            ext_ref[pad:pad + n_rows, ln] = x
            xc = conv_b[:, ln] + conv_w[CONV_WIDTH - 1][:, ln] * x
            for i in range(CONV_WIDTH - 1):
                xc = xc + conv_w[i][:, ln] * ext_ref[pl.ds(pad - (CONV_WIDTH - 1) + i, n_rows), ln]
            ext_ref[0:pad, ln] = ext_ref[n_rows:n_rows + pad, ln]
            z = _dot(xc.astype(BF16), wbd_ref[s])
            r = _sigmoid(z[:, :LANES] + b_a[:, ln])
            i_gate = _sigmoid(z[:, LANES:] + b_x[:, ln])
            log_a = (-LRU_C) * r * _softplus(-lam[:, ln])
            a = jnp.exp(log_a)
            val[s] = (a, jnp.sqrt(-jnp.tanh(log_a) * (a * a + 1.0)) * (i_gate * xc))

        n_groups = n_rows // 8
        scan_parts = 4

        def scan(s, part):
            ln = lane_cols[s]
            a, u = val[s]
            if part == 0:
                val[s, "hc"] = hc_ref[:, ln]
                val[s, "h"] = []
            hc = val[s, "hc"]
            for g in range(part * n_groups // scan_parts, (part + 1) * n_groups // scan_parts):
                aa, uu = a[8 * g:8 * g + 8], u[8 * g:8 * g + 8]
                for sh in (1, 2, 4):
                    ok = row8 >= sh
                    uu = jnp.where(ok, aa * pltpu.roll(uu, sh, 0) + uu, uu)
                    aa = jnp.where(ok, aa * pltpu.roll(aa, sh, 0), aa)
                hh = aa * hc + uu
                val[s, "h"].append(hh)
                hc = jnp.broadcast_to(hh[7:8, :], (8, LANES))
            val[s, "hc"] = hc
            if part == scan_parts - 1:
                hc_ref[:, ln] = val.pop((s, "hc"))
                del val[s]

        def gate(s):
            ln = lane_cols[s]
            gt = lg_ref[:, ln].astype(F32)
            gelu = 0.5 * gt * (1.0 + jnp.tanh(0.7978845608028654 * (gt + 0.044715 * gt * gt * gt)))
            y = jnp.concatenate(val.pop((s, "h")), axis=0) * gelu
            yl_ref[:, ln] = y
            ssq_ref[...] = y * y if s == 0 else ssq_ref[...] + y * y

        def finish():
            ms = jnp.sum(ssq_ref[...], axis=-1, keepdims=True) * (1.0 / width)
            scale = lax.rsqrt(ms + NORM_EPS)
            for ln in lane_cols:
                ol_ref[:, ln] = (yl_ref[:, ln] * scale * norm_g[:, ln]).astype(ol_ref.dtype)

        ops = []
        for s in range(len(lane_cols)):
            ops += [functools.partial(gates, s)]
            ops += [functools.partial(scan, s, part) for part in range(scan_parts)]
            ops += [functools.partial(gate, s)]
        return ops + [finish]

    def interleave(main, fill):
        n, m = len(main), len(fill)
        done = 0
        for i, op in enumerate(main):
            op()
            while done < m and (done + 1) * n <= (i + 1) * (m + 1):
                fill[done]()
                done += 1
        for op in fill[done:]:
            op()

    slot_sets = ((0, 1), (2, 3))

    def step(parity):
        chain_slots, prep_slots = slot_sets[parity], slot_sets[1 - parity]
        free0, bound0 = chain_ops(chain_slots[0], 0, parity)
        free1, bound1 = chain_ops(chain_slots[1], 1, parity)
        main = [op for pair in zip(free0, free1) for stage in pair for op in stage]
        tail = [op for stage in bound0 + bound1 for op in stage]
        lru = lru_ops()
        fill = []
        for half in range(2):
            fill += post_ops(prep_slots[half], half, 1 - parity)
            fill += prep_ops(xn_ref, sn_ref, halves[half], prep_slots[half])
            fill += lru[half * len(lru) // 2:(half + 1) * len(lru) // 2]
        cut = len(fill) // 2
        interleave(main, fill[:cut])
        interleave(tail, fill[cut:])

    @pl.when(j == 0)
    def _():
        y_ref[1] = jnp.zeros_like(y_ref[1])
        for slot in slot_sets[1]:
            bon_ref[slot] = jnp.zeros_like(bon_ref[slot])
            gg_ref[slot] = jnp.zeros_like(gg_ref[slot])
        for op in prep_ops(x0_ref, s0_ref, halves[0], 0) + prep_ops(x0_ref, s0_ref, halves[1], 1):
            op()

    @pl.when((2 * j) % chunks_per_seq == 0)
    def _():
        hs_ref[...] = jnp.zeros_like(hs_ref)
        hc_ref[...] = jnp.zeros_like(hc_ref)
        ext_ref[0:8, :] = jnp.zeros((8, width), F32)

    for parity in range(2):
        @pl.when((j % 2 == parity) & (j < n_steps))
        def _():
            step(parity)

    @pl.when(j == n_steps)
    def _():
        last = n_steps % 2
        for half in range(2):
            for op in post_ops(slot_sets[1 - last][half], half, 1 - last):
                op()


def _mixers(p, vec, w2p, a2p, g2p, lvec, w_bd, *, width, lru_x_col, lru_gate_col, small_col, small_w,
            chunks_per_seq):
    t = p.shape[0]
    n_chunks = t // CHUNK
    assert chunks_per_seq % 2 == 0 and n_chunks % chunks_per_seq == 0
    n_slabs = width // SLAB
    n_steps = n_chunks // 2
    block = 2 * CHUNK
    views = []
    for idx in (lambda j: 0, lambda j: jnp.minimum(j + 1, n_steps - 1)):
        views.append(pl.BlockSpec((block, 3 * width), lambda j, idx=idx: (idx(j), 0)))
        views.append(pl.BlockSpec((block, small_w), lambda j, idx=idx: (idx(j), small_col)))
    current = lambda j: jnp.minimum(j, n_steps - 1)
    views.append(pl.BlockSpec((block, width), lambda j: (current(j), lru_x_col)))
    views.append(pl.BlockSpec((block, width), lambda j: (current(j), lru_gate_col)))
    full = lambda a: pl.BlockSpec(a.shape, lambda j: (0,) * a.ndim)
    n_slots = 4
    slot_bf16 = pltpu.VMEM((n_slots, CHUNK, width), BF16)
    slot_f32 = pltpu.VMEM((n_slots, CHUNK, width), F32)
    return pl.pallas_call(
        functools.partial(_mixer_kernel, chunks_per_seq=chunks_per_seq, n_steps=n_steps),
        grid=(n_steps + 1,),
        in_specs=views + [full(vec), full(w2p), full(a2p), full(g2p), full(lvec), full(w_bd)],
        out_specs=[pl.BlockSpec((block, width), lambda j: (jnp.maximum(j - 1, 0), 0)),
                   pl.BlockSpec((block, width), lambda j: (current(j), 0))],
        out_shape=[jax.ShapeDtypeStruct((t, width), BF16), jax.ShapeDtypeStruct((t, width), BF16)],
        scratch_shapes=[
            pltpu.VMEM((n_slabs, SLAB, SLAB), F32),
            slot_bf16, slot_bf16, slot_bf16, slot_bf16, slot_bf16, slot_bf16, slot_bf16,
            pltpu.VMEM((n_slots, 8, width), F32),
            slot_f32, slot_f32,
            pltpu.VMEM((2, 2, CHUNK, width), F32),
            pltpu.VMEM((2, 2, n_slabs, SLAB, SLAB), BF16),
            pltpu.VMEM((2, n_slabs, SLAB, SLAB), BF16),
            pltpu.VMEM((2, n_slabs, SLAB, SLAB), BF16),
            pltpu.VMEM((2, n_slabs, SLAB, SLAB), BF16), pltpu.VMEM((2, n_slabs, SLAB, SLAB), BF16),
            pltpu.VMEM((2, n_slabs, SLAB, SLAB), BF16),
            pltpu.VMEM((block + 8, width), F32),
            pltpu.VMEM((8, width), F32),
            pltpu.VMEM((block, width), F32),
            pltpu.VMEM((block, LANES), F32),
        ],
        compiler_params=pltpu.CompilerParams(
            dimension_semantics=("arbitrary",), vmem_limit_bytes=VMEM_LIMIT),
        name="mixers",
    )(p, p, p, p, p, p, vec, w2p, a2p, g2p, lvec, w_bd)


def _pad_rows(w, rows, at):
    out = jnp.zeros((rows, w.shape[1]), w.dtype)
    return out.at[at:at + w.shape[0]].set(w)


def _block_diag_pairs(wa, wx):
    def bd(w):
        h, n, _ = w.shape
        w = w.reshape(h // PAIR, PAIR, n, n)
        z = jnp.zeros_like(w[:, 0])
        top = jnp.concatenate([w[:, 0], z], axis=2)
        bot = jnp.concatenate([z, w[:, 1]], axis=2)
        return jnp.concatenate([top, bot], axis=1)
    return jnp.concatenate([bd(wa), bd(wx)], axis=2)


def kernel(x, ffn1_norm, ffn1_w_gate, ffn1_w_up, ffn1_w_down, mix_norm, w_in, rwkv_mu, rwkv_w0, rwkv_w2, rwkv_a0, rwkv_a2, rwkv_g2, rwkv_k_k, rwkv_k_a, rwkv_r_k, rwkv_ln_w, rwkv_ln_b, lru_conv_w, lru_conv_b, lru_wa, lru_ba, lru_wx, lru_bx, lru_lam, lru_norm, w_out, ffn2_norm, ffn2_w_gate, ffn2_w_up, ffn2_w_down, final_norm):
    bsz, seq, d = x.shape
    depth = w_in.shape[0]
    wr = rwkv_w0.shape[1]
    wl = lru_lam.shape[1]
    n_w, n_a, n_g = rwkv_w2.shape[1], rwkv_a2.shape[1], rwkv_g2.shape[1]
    assert wr == wl and wr % SLAB == 0 and n_w + n_a == LANES and lru_wa.shape[2] == HEAD_DIM
    small_w = 4 * LANES
    g_pad = 2 * LANES
    assert n_g <= g_pad and (3 * wr) % small_w == 0 and seq % (2 * CHUNK) == 0

    row = lambda v: v.reshape(1, -1).astype(F32)
    h = x.reshape(bsz * seq, d)
    for l in range(depth):
        h = _ffn(h, row(ffn1_norm[l]), ffn1_w_gate[l].astype(BF16), ffn1_w_up[l].astype(BF16),
                 ffn1_w_down[l].astype(BF16), row(final_norm), final_norm=False)

        o_r, o_w = 0, wr
        o_k = o_w + n_w
        o_v = o_k + wr
        o_a = o_v + wr
        o_g = o_a + n_a
        o_l = o_g + n_g
        w = w_in[l]
        zpad = jnp.zeros((d, small_w - n_w - n_a - n_g), BF16)
        w_re = jnp.concatenate([w[:, o_r:o_r + wr].astype(BF16), w[:, o_k:o_k + wr].astype(BF16),
                                w[:, o_v:o_v + wr].astype(BF16), w[:, o_l:o_l + 2 * wl].astype(BF16),
                                w[:, o_w:o_w + n_w].astype(BF16), w[:, o_a:o_a + n_a].astype(BF16),
                                w[:, o_g:o_g + n_g].astype(BF16), zpad], axis=1)
        mu = rwkv_mu[l]
        mu_re = jnp.concatenate([mu[o_r:o_r + wr], mu[o_k:o_k + wr], mu[o_v:o_v + wr],
                                 jnp.zeros((2 * wl,), F32),
                                 mu[o_w:o_w + n_w], mu[o_a:o_a + n_a], mu[o_g:o_g + n_g],
                                 jnp.zeros((small_w - n_w - n_a - n_g,), F32)]).reshape(1, -1).astype(F32)
        vec = jnp.stack([rwkv_w0[l], rwkv_a0[l], rwkv_k_k[l], rwkv_k_a[l], rwkv_r_k[l].reshape(-1),
                         rwkv_ln_w[l], rwkv_ln_b[l], jnp.zeros((wr,), F32)]).astype(F32)
        w2p = _pad_rows(rwkv_w2[l], LANES, 0).astype(BF16)
        a2p = _pad_rows(rwkv_a2[l], LANES, n_w).astype(BF16)
        g2p = _pad_rows(rwkv_g2[l], g_pad, 0).astype(BF16)

        lvec = jnp.concatenate([lru_conv_w[l], jnp.stack([lru_conv_b[l], lru_ba[l], lru_bx[l],
                                                          lru_lam[l], lru_norm[l]]),
                                jnp.zeros((16 - CONV_WIDTH - 5, wl), F32)], axis=0).astype(F32)
        w_bd = _block_diag_pairs(lru_wa[l], lru_wx[l]).astype(BF16)

        p = _proj(h, row(mix_norm[l]), w_re, mu_re, rows_per_seq=seq)
        y_r, y_l = _mixers(p, vec, w2p, a2p, g2p, lvec, w_bd, width=wr, lru_x_col=3, lru_gate_col=4,
                           small_col=(3 * wr + 2 * wl) // small_w, small_w=small_w,
                           chunks_per_seq=seq // CHUNK)
        h = _outproj(h, y_r, y_l, w_out[l].astype(BF16))
        h = _ffn(h, row(ffn2_norm[l]), ffn2_w_gate[l].astype(BF16), ffn2_w_up[l].astype(BF16),
                 ffn2_w_down[l].astype(BF16), row(final_norm), final_norm=(l == depth - 1))
    return h.reshape(bsz, seq, d)
```

```python
import functools

import jax
import jax.numpy as jnp
from jax import lax
from jax.experimental import pallas as pl
from jax.experimental.pallas import tpu as pltpu

F32 = jnp.float32
BF16 = jnp.bfloat16

HEAD_DIM = 64
CONV_WIDTH = 4
LRU_C = 8.0
NORM_EPS = 1e-6
DECAY_SCALE = 0.6065306597126334
GN_EPS = 64e-5

LANES = 128
CHUNK = 64
PAIR = LANES // HEAD_DIM
SLAB = 256
VMEM_PHYSICAL_V7X = 64 * 1024 * 1024
VMEM_LIMIT = VMEM_PHYSICAL_V7X - 4 * 1024 * 1024
VMEM_LIMIT_FFN = VMEM_PHYSICAL_V7X - 1024 * 1024


def _dot(a, b):
    return jnp.dot(a, b, preferred_element_type=F32)


def _dot_nt(a, b):
    return lax.dot_general(a, b, (((1,), (1,)), ((), ())), preferred_element_type=F32)


def _dot_tn(a, b):
    return lax.dot_general(a, b, (((0,), (0,)), ((), ())), preferred_element_type=F32)


def _sigmoid(x):
    return 0.5 * (jnp.tanh(0.5 * x) + 1.0)


def _softplus(x):
    return jnp.maximum(x, 0.0) + jnp.log(1.0 + jnp.exp(-jnp.abs(x)))


def _rms_norm(x, g):
    ms = jnp.mean(x * x, axis=-1, keepdims=True)
    return x * lax.rsqrt(ms + NORM_EPS) * g


def _ffn_kernel(h_ref, g_ref, wg_ref, wu_ref, wd_ref, fg_ref, o_ref, xn_ref, *, final_norm):
    j = pl.program_id(1)

    @pl.when(j == 0)
    def _():
        xn_ref[...] = _rms_norm(h_ref[...], g_ref[...]).astype(BF16)
        o_ref[...] = jnp.zeros_like(o_ref)

    xn = xn_ref[...]
    gate = _dot(xn, wg_ref[...])
    up = _dot(xn, wu_ref[...])
    act = (gate * _sigmoid(gate) * up).astype(BF16)
    o_ref[...] += _dot(act, wd_ref[...])

    @pl.when(j == pl.num_programs(1) - 1)
    def _():
        out = h_ref[...] + 0.5 * o_ref[...]
        if final_norm:
            out = _rms_norm(out, fg_ref[...])
        o_ref[...] = out


def _ffn(h, g, wg, wu, wd, fg, *, final_norm, tm=1024, tf=512):
    t, d = h.shape
    dff = wg.shape[1]
    grid = (t // tm, dff // tf)
    return pl.pallas_call(
        functools.partial(_ffn_kernel, final_norm=final_norm),
        grid=grid,
        in_specs=[
            pl.BlockSpec((tm, d), lambda i, j: (i, 0)),
            pl.BlockSpec((1, d), lambda i, j: (0, 0)),
            pl.BlockSpec((d, tf), lambda i, j: (0, j)),
            pl.BlockSpec((d, tf), lambda i, j: (0, j)),
            pl.BlockSpec((tf, d), lambda i, j: (j, 0)),
            pl.BlockSpec((1, d), lambda i, j: (0, 0)),
        ],
        out_specs=pl.BlockSpec((tm, d), lambda i, j: (i, 0)),
        out_shape=jax.ShapeDtypeStruct((t, d), F32),
        scratch_shapes=[pltpu.VMEM((tm, d), BF16)],
        compiler_params=pltpu.CompilerParams(
            dimension_semantics=("parallel", "arbitrary"), vmem_limit_bytes=VMEM_LIMIT_FFN),
        name="ffn",
    )(h, g, wg, wu, wd, fg)


def _proj_kernel(h_ref, g_ref, w_ref, mu_ref, o_ref, xn_ref, last_ref, *, tiles_per_seq):
    i, j = pl.program_id(0), pl.program_id(1)

    @pl.when(j == 0)
    def _():
        xn_ref[...] = _rms_norm(h_ref[...], g_ref[...]).astype(BF16)

    @pl.when(i == 0)
    def _():
        last_ref[j] = jnp.zeros(last_ref.shape[1:], F32)

    z = _dot(xn_ref[...], w_ref[...])
    above = jnp.where(i % tiles_per_seq == 0, 0.0, last_ref[j, 0:1, :])
    last_ref[j, 0:1, :] = z[z.shape[0] - 1:, :]
    o_ref[...] = (z + (_shift_rows(z, above) - z) * mu_ref[...]).astype(o_ref.dtype)


def _proj(h, g, w, mu, *, rows_per_seq, tm=512, tn=2816):
    t, d = h.shape
    n = w.shape[1]
    assert rows_per_seq % tm == 0
    return pl.pallas_call(
        functools.partial(_proj_kernel, tiles_per_seq=rows_per_seq // tm),
        grid=(t // tm, n // tn),
        in_specs=[
            pl.BlockSpec((tm, d), lambda i, j: (i, 0)),
            pl.BlockSpec((1, d), lambda i, j: (0, 0)),
            pl.BlockSpec((d, tn), lambda i, j: (0, j)),
            pl.BlockSpec((1, tn), lambda i, j: (0, j)),
        ],
        out_specs=pl.BlockSpec((tm, tn), lambda i, j: (i, j)),
        out_shape=jax.ShapeDtypeStruct((t, n), BF16),
        scratch_shapes=[pltpu.VMEM((tm, d), BF16),
                        pltpu.VMEM((n // tn, 8, tn), F32)],
        compiler_params=pltpu.CompilerParams(
            dimension_semantics=("arbitrary", "arbitrary"), vmem_limit_bytes=VMEM_LIMIT),
        name="proj",
    )(h, g, w, mu)


def _outproj_kernel(h_ref, ya_ref, yb_ref, w_ref, o_ref):
    ka = ya_ref.shape[1]
    o_ref[...] = (h_ref[...] + _dot(ya_ref[...], w_ref[:ka, :]) + _dot(yb_ref[...], w_ref[ka:, :]))


def _outproj(h, ya, yb, w, *, tm=512):
    t, d = h.shape
    ka, kb = ya.shape[1], yb.shape[1]
    return pl.pallas_call(
        _outproj_kernel,
        grid=(t // tm,),
        in_specs=[
            pl.BlockSpec((tm, d), lambda i: (i, 0)),
            pl.BlockSpec((tm, ka), lambda i: (i, 0)),
            pl.BlockSpec((tm, kb), lambda i: (i, 0)),
            pl.BlockSpec((ka + kb, d), lambda i: (0, 0)),
        ],
        out_specs=pl.BlockSpec((tm, d), lambda i: (i, 0)),
        out_shape=jax.ShapeDtypeStruct((t, d), F32),
        compiler_params=pltpu.CompilerParams(
            dimension_semantics=("parallel",), vmem_limit_bytes=VMEM_LIMIT),
        name="outproj",
    )(h, ya, yb, w)


def _head_sum(x, ones_bd, exact):
    hi = x.astype(BF16)
    if not exact:
        return _dot(hi, ones_bd)
    r = x.shape[0]
    lo = (x - hi.astype(F32)).astype(BF16)
    z = _dot(jnp.concatenate([hi, lo], axis=0), ones_bd)
    return z[:r] + z[r:]


def _shift_rows(x, carry_row):
    rolled = pltpu.roll(x, 1, 0)
    row = lax.broadcasted_iota(jnp.int32, x.shape, 0)
    return jnp.where(row == 0, carry_row, rolled)


def _mixer_kernel(x0_ref, s0_ref, xn_ref, sn_ref, lx_ref, lg_ref,
                  vec_ref, w2_ref, a2_ref, g2_ref, lvec_ref, wbd_ref,
                  o_ref, ol_ref,
                  hs_ref, ab_ref, bt_ref, kt_ref, rb_ref, vv_ref, bh_ref, kh_ref,
                  pc_ref, bon_ref, gg_ref, y_ref, pw_ref, tb_ref, pi_ref, aak_ref, arb_ref, ark_ref,
                  ext_ref, hc_ref, yl_ref, ssq_ref,
                  *, chunks_per_seq, n_steps):
    j = pl.program_id(0)
    width = o_ref.shape[1]
    n_slabs = width // SLAB
    heads_per_slab = SLAB // HEAD_DIM

    vec = vec_ref[...]
    w0, a0, k_k, k_a, r_k, ln_w, ln_b = (vec[i:i + 1, :] for i in range(7))

    lane_s = lax.broadcasted_iota(jnp.int32, (SLAB, SLAB), 1)
    sub_s = lax.broadcasted_iota(jnp.int32, (SLAB, SLAB), 0)
    strict = lane_s < sub_s
    incl = lane_s <= sub_s
    eye = jnp.where(lane_s == sub_s, 1.0, 0.0).astype(BF16)
    ones_bd = jnp.where((lane_s // HEAD_DIM) == (sub_s // HEAD_DIM), 1.0, 0.0).astype(BF16)
    head_of_lane = lax.broadcasted_iota(jnp.int32, (CHUNK, SLAB), 1) // HEAD_DIM
    row_c = lax.broadcasted_iota(jnp.int32, (CHUNK, SLAB), 0)
    cols = [slice(q * SLAB, (q + 1) * SLAB) for q in range(n_slabs)]
    slabs = range(n_slabs)
    n_sq = CHUNK.bit_length() - 2
    halves = (slice(0, CHUNK), slice(CHUNK, 2 * CHUNK))

    def prep_ops(x_ref, s_ref, rows, slot):
        small = {}

        def lora_inputs():
            s1 = s_ref[rows, 0:LANES].astype(F32)
            s2 = s_ref[rows, LANES:LANES + g2_ref.shape[0]].astype(F32)
            small["tanh_w"] = jnp.tanh(s1).astype(BF16)
            small["lin_a"] = s_ref[rows, 0:LANES]
            small["sig_g"] = _sigmoid(s2).astype(BF16)

        val = {}

        def load(q):
            lo = q * SLAB
            val[q, "r"] = x_ref[rows, lo:lo + SLAB].astype(F32)
            val[q, "k"] = x_ref[rows, width + lo:width + lo + SLAB].astype(F32)
            vv_ref[slot, :, cols[q]] = x_ref[rows, 2 * width + lo:2 * width + lo + SLAB]
            val[q, "v"] = x_ref[rows, 2 * width + lo:2 * width + lo + SLAB].astype(F32)

        def lora(q):
            c = cols[q]
            z = w0[:, c] + _dot(small["tanh_w"], w2_ref[:, c])
            val[q, "a"] = _sigmoid(a0[:, c] + _dot(small["lin_a"], a2_ref[:, c]))
            gg_ref[slot, :, c] = _dot(small["sig_g"], g2_ref[:, c])
            val[q, "ld"] = (-DECAY_SCALE) * _sigmoid(z)

        def keys(q):
            c = cols[q]
            k = val[q, "k"]
            kk = k * k_k[:, c]
            val[q, "kk"] = kk * lax.rsqrt(jnp.maximum(_head_sum(kk * kk, ones_bd, exact=True), 1e-24))
            val[q, "k"] = k * (1.0 + (val[q, "a"] - 1.0) * k_a[:, c])

        def bonus(q):
            c = cols[q]
            rk = val[q, "r"] * val[q, "k"] * r_k[:, c]
            bon_ref[slot, :, c] = _head_sum(rk, ones_bd, exact=False) * val.pop((q, "v"))

        def decay(q):
            cum = val[q, "ld"]
            sh = 1
            while sh < CHUNK:
                cum = cum + jnp.where(row_c >= sh, pltpu.roll(cum, sh, 0), 0.0)
                sh *= 2
            val[q, "cum"] = cum

        def scale_in(q):
            c = cols[q]
            cum, ld, kk, r = val[q, "cum"], val.pop((q, "ld")), val[q, "kk"], val.pop((q, "r"))
            ab_ref[slot, :, c] = (-kk * jnp.exp(cum - ld)).astype(BF16)
            rb_ref[slot, :, c] = (r * jnp.exp(cum)).astype(BF16)

        def scale_out(q):
            c = cols[q]
            cum, kk, a, k = val.pop((q, "cum")), val.pop((q, "kk")), val.pop((q, "a")), val.pop((q, "k"))
            tot = cum[CHUNK - 1:CHUNK, :]
            kb = kk * a
            e_inv = jnp.exp(-cum)
            e_end = jnp.exp(tot - cum)
            bt_ref[slot, :, c] = (kb * e_inv).astype(BF16)
            kt_ref[slot, :, c] = (k * e_inv).astype(BF16)
            bh_ref[slot, :, c] = (kb * e_end).astype(BF16)
            kh_ref[slot, :, c] = (k * e_end).astype(BF16)
            pc_ref[slot, 0:1, c] = jnp.exp(tot)

        stages = (load, lora, keys, bonus, decay, scale_in, scale_out)
        return [lora_inputs] + [functools.partial(stage, q) for q in slabs for stage in stages]

    def stack(x):
        zero = jnp.zeros_like(x)
        return jnp.concatenate([jnp.where(head_of_lane == hd, x, zero) for hd in range(heads_per_slab)],
                               axis=0)

    def chain_ops(slot, half, ybuf):
        live = {}

        def gram(q):
            xx = jnp.concatenate([stack(ab_ref[slot, :, cols[q]]), stack(rb_ref[slot, :, cols[q]])], axis=0)
            gb = _dot_nt(xx, stack(bt_ref[slot, :, cols[q]]))
            gk = _dot_nt(xx, stack(kt_ref[slot, :, cols[q]]))
            a_ab = jnp.where(strict, gb[:SLAB], 0.0).astype(BF16)
            pw_ref[half, 0, q] = a_ab
            tb_ref[half, q] = a_ab + eye
            aak_ref[half, q] = jnp.where(strict, gk[:SLAB], 0.0).astype(BF16)
            arb_ref[half, q] = jnp.where(incl, gb[SLAB:], 0.0).astype(BF16)
            ark_ref[half, q] = jnp.where(incl, gk[SLAB:], 0.0).astype(BF16)

        def square(s, q):
            p2 = _dot(pw_ref[half, s % 2, q], pw_ref[half, s % 2, q]).astype(BF16)
            pw_ref[half, (s + 1) % 2, q] = p2
            pi_ref[half, q] = p2 + eye

        def extend(s, q):
            tb_ref[half, q] = _dot(tb_ref[half, q], pi_ref[half, q]).astype(BF16)

        def project(q):
            vs = stack(vv_ref[slot, :, cols[q]])
            xx = jnp.concatenate([stack(ab_ref[slot, :, cols[q]]), stack(rb_ref[slot, :, cols[q]])], axis=0)
            live[q] = (vs, _dot_nt(xx, hs_ref[q].astype(BF16)))

        def rhs(q):
            vs, xh = live[q]
            live[q] = (vs, xh[SLAB:], (xh[:SLAB] + _dot(aak_ref[half, q], vs)).astype(BF16))

        def solve(q):
            vs, xrh, b = live[q]
            live[q] = (vs, xrh, _dot(tb_ref[half, q], b).astype(BF16))

        def emit(q):
            vs, xrh, u = live[q]
            y_st = xrh + _dot(jnp.concatenate([arb_ref[half, q], ark_ref[half, q]], axis=1),
                              jnp.concatenate([u, vs], axis=0))
            y = y_st[0:CHUNK]
            for hd in range(1, heads_per_slab):
                y = y + y_st[hd * CHUNK:(hd + 1) * CHUNK]
            y_ref[ybuf, half, :, cols[q]] = y

        def advance(q):
            vs, _, u = live.pop(q)
            pc = pc_ref[slot, 0:1, cols[q]]
            hs_ref[q] = hs_ref[q] * pc + _dot_tn(
                jnp.concatenate([u, vs], axis=0),
                jnp.concatenate([stack(bh_ref[slot, :, cols[q]]), stack(kh_ref[slot, :, cols[q]])], axis=0))

        free = [gram]
        for s in range(n_sq):
            free += [functools.partial(square, s), functools.partial(extend, s)]
        bound = [project, rhs, solve, emit, advance]
        per_slab = lambda stages: [[functools.partial(stage, q) for q in slabs] for stage in stages]
        return per_slab(free), per_slab(bound)

    def post_ops(slot, half, ybuf):
        val = {}
        inv_n = 1.0 / HEAD_DIM

        def center(q):
            y = y_ref[ybuf, half, :, cols[q]]
            val[q] = y - _head_sum(y, ones_bd, exact=False) * inv_n

        def scale(q):
            c = cols[q]
            yc = val.pop(q)
            var = _head_sum(yc * yc, ones_bd, exact=False) * inv_n
            yn = yc * lax.rsqrt(var + GN_EPS) * ln_w[:, c] + ln_b[:, c]
            o_ref[halves[half], c] = ((yn + bon_ref[slot, :, c]) * gg_ref[slot, :, c]).astype(o_ref.dtype)

        return [functools.partial(stage, q) for q in slabs for stage in (center, scale)]

    def lru_ops():
        n_rows = 2 * CHUNK
        pad = 8
        lvec = lvec_ref[...]
        conv_w = [lvec[i:i + 1, :] for i in range(CONV_WIDTH)]
        conv_b, b_a, b_x, lam, norm_g = (lvec[i:i + 1, :] for i in range(CONV_WIDTH, CONV_WIDTH + 5))
        row8 = lax.broadcasted_iota(jnp.int32, (8, LANES), 0)
        lane_cols = [slice(s * LANES, (s + 1) * LANES) for s in range(width // LANES)]
        val = {}

        def gates(s):
            ln = lane_cols[s]
            x = lx_ref[:, ln].astype(F32)
            ext_ref[pad:pad + n_rows, ln] = x
            xc = conv_b[:, ln] + conv_w[CONV_WIDTH - 1][:, ln] * x
            for i in range(CONV_WIDTH - 1):
                xc = xc + conv_w[i][:, ln] * ext_ref[pl.ds(pad - (CONV_WIDTH - 1) + i, n_rows), ln]
            ext_ref[0:pad, ln] = ext_ref[n_rows:n_rows + pad, ln]
            z = _dot(xc.astype(BF16), wbd_ref[s])
            r = _sigmoid(z[:, :LANES] + b_a[:, ln])
            i_gate = _sigmoid(z[:, LANES:] + b_x[:, ln])
            log_a = (-LRU_C) * r * _softplus(-lam[:, ln])
            a = jnp.exp(log_a)
            val[s] = (a, jnp.sqrt(-jnp.tanh(log_a) * (a * a + 1.0)) * (i_gate * xc))

        n_groups = n_rows // 8
        scan_parts = 4

        def scan(s, part):
            ln = lane_cols[s]
            a, u = val[s]
            if part == 0:
                val[s, "hc"] = hc_ref[:, ln]
                val[s, "h"] = []
            hc = val[s, "hc"]
            for g in range(part * n_groups // scan_parts, (part + 1) * n_groups // scan_parts):
                aa, uu = a[8 * g:8 * g + 8], u[8 * g:8 * g + 8]
                for sh in (1, 2, 4):
                    ok = row8 >= sh
                    uu = jnp.where(ok, aa * pltpu.roll(uu, sh, 0) + uu, uu)
                    aa = jnp.where(ok, aa * pltpu.roll(aa, sh, 0), aa)
                hh = aa * hc + uu
                val[s, "h"].append(hh)
                hc = jnp.broadcast_to(hh[7:8, :], (8, LANES))
            val[s, "hc"] = hc
            if part == scan_parts - 1:
                hc_ref[:, ln] = val.pop((s, "hc"))
                del val[s]

        def gate(s):
            ln = lane_cols[s]
            gt = lg_ref[:, ln].astype(F32)
            gelu = 0.5 * gt * (1.0 + jnp.tanh(0.7978845608028654 * (gt + 0.044715 * gt * gt * gt)))
            y = jnp.concatenate(val.pop((s, "h")), axis=0) * gelu
            yl_ref[:, ln] = y
            ssq_ref[...] = y * y if s == 0 else ssq_ref[...] + y * y

        def finish():
            ms = jnp.sum(ssq_ref[...], axis=-1, keepdims=True) * (1.0 / width)
            scale = lax.rsqrt(ms + NORM_EPS)
            for ln in lane_cols:
                ol_ref[:, ln] = (yl_ref[:, ln] * scale * norm_g[:, ln]).astype(ol_ref.dtype)

        ops = []
        for s in range(len(lane_cols)):
            ops += [functools.partial(gates, s)]
            ops += [functools.partial(scan, s, part) for part in range(scan_parts)]
            ops += [functools.partial(gate, s)]
        return ops + [finish]

    def interleave(main, fill):
        n, m = len(main), len(fill)
        done = 0
        for i, op in enumerate(main):
            op()
            while done < m and (done + 1) * n <= (i + 1) * (m + 1):
                fill[done]()
                done += 1
        for op in fill[done:]:
            op()

    slot_sets = ((0, 1), (2, 3))

    def step(parity):
        chain_slots, prep_slots = slot_sets[parity], slot_sets[1 - parity]
        free0, bound0 = chain_ops(chain_slots[0], 0, parity)
        free1, bound1 = chain_ops(chain_slots[1], 1, parity)
        main = [op for pair in zip(free0, free1) for stage in pair for op in stage]
        tail = [op for stage in bound0 + bound1 for op in stage]
        lru = lru_ops()
        fill = []
        for half in range(2):
            fill += post_ops(prep_slots[half], half, 1 - parity)
            fill += prep_ops(xn_ref, sn_ref, halves[half], prep_slots[half])
            fill += lru[half * len(lru) // 2:(half + 1) * len(lru) // 2]
        cut = len(fill) // 2
        interleave(main, fill[:cut])
        interleave(tail, fill[cut:])

    @pl.when(j == 0)
    def _():
        y_ref[1] = jnp.zeros_like(y_ref[1])
        for slot in slot_sets[1]:
            bon_ref[slot] = jnp.zeros_like(bon_ref[slot])
            gg_ref[slot] = jnp.zeros_like(gg_ref[slot])
        for op in prep_ops(x0_ref, s0_ref, halves[0], 0) + prep_ops(x0_ref, s0_ref, halves[1], 1):
            op()

    @pl.when((2 * j) % chunks_per_seq == 0)
    def _():
        hs_ref[...] = jnp.zeros_like(hs_ref)
        hc_ref[...] = jnp.zeros_like(hc_ref)
        ext_ref[0:8, :] = jnp.zeros((8, width), F32)

    for parity in range(2):
        @pl.when((j % 2 == parity) & (j < n_steps))
        def _():
            step(parity)

    @pl.when(j == n_steps)
    def _():
        last = n_steps % 2
        for half in range(2):
            for op in post_ops(slot_sets[1 - last][half], half, 1 - last):
                op()


def _mixers(p, vec, w2p, a2p, g2p, lvec, w_bd, *, width, lru_x_col, lru_gate_col, small_col, small_w,
            chunks_per_seq):
    t = p.shape[0]
    n_chunks = t // CHUNK
    assert chunks_per_seq % 2 == 0 and n_chunks % chunks_per_seq == 0
    n_slabs = width // SLAB
    n_steps = n_chunks // 2
    block = 2 * CHUNK
    views = []
    for idx in (lambda j: 0, lambda j: jnp.minimum(j + 1, n_steps - 1)):
        views.append(pl.BlockSpec((block, 3 * width), lambda j, idx=idx: (idx(j), 0)))
        views.append(pl.BlockSpec((block, small_w), lambda j, idx=idx: (idx(j), small_col)))
    current = lambda j: jnp.minimum(j, n_steps - 1)
    views.append(pl.BlockSpec((block, width), lambda j: (current(j), lru_x_col)))
    views.append(pl.BlockSpec((block, width), lambda j: (current(j), lru_gate_col)))
    full = lambda a: pl.BlockSpec(a.shape, lambda j: (0,) * a.ndim)
    n_slots = 4
    slot_bf16 = pltpu.VMEM((n_slots, CHUNK, width), BF16)
    slot_f32 = pltpu.VMEM((n_slots, CHUNK, width), F32)
    return pl.pallas_call(
        functools.partial(_mixer_kernel, chunks_per_seq=chunks_per_seq, n_steps=n_steps),
        grid=(n_steps + 1,),
        in_specs=views + [full(vec), full(w2p), full(a2p), full(g2p), full(lvec), full(w_bd)],
        out_specs=[pl.BlockSpec((block, width), lambda j: (jnp.maximum(j - 1, 0), 0)),
                   pl.BlockSpec((block, width), lambda j: (current(j), 0))],
        out_shape=[jax.ShapeDtypeStruct((t, width), BF16), jax.ShapeDtypeStruct((t, width), BF16)],
        scratch_shapes=[
            pltpu.VMEM((n_slabs, SLAB, SLAB), F32),
            slot_bf16, slot_bf16, slot_bf16, slot_bf16, slot_bf16, slot_bf16, slot_bf16,
            pltpu.VMEM((n_slots, 8, width), F32),
            slot_f32, slot_f32,
            pltpu.VMEM((2, 2, CHUNK, width), F32),
            pltpu.VMEM((2, 2, n_slabs, SLAB, SLAB), BF16),
            pltpu.VMEM((2, n_slabs, SLAB, SLAB), BF16),
            pltpu.VMEM((2, n_slabs, SLAB, SLAB), BF16),
            pltpu.VMEM((2, n_slabs, SLAB, SLAB), BF16), pltpu.VMEM((2, n_slabs, SLAB, SLAB), BF16),
            pltpu.VMEM((2, n_slabs, SLAB, SLAB), BF16),
            pltpu.VMEM((block + 8, width), F32),
            pltpu.VMEM((8, width), F32),
            pltpu.VMEM((block, width), F32),
            pltpu.VMEM((block, LANES), F32),
        ],
        compiler_params=pltpu.CompilerParams(
            dimension_semantics=("arbitrary",), vmem_limit_bytes=VMEM_LIMIT),
        name="mixers",
    )(p, p, p, p, p, p, vec, w2p, a2p, g2p, lvec, w_bd)


def _pad_rows(w, rows, at):
    out = jnp.zeros((rows, w.shape[1]), w.dtype)
    return out.at[at:at + w.shape[0]].set(w)


def _block_diag_pairs(wa, wx):
    def bd(w):
        h, n, _ = w.shape
        w = w.reshape(h // PAIR, PAIR, n, n)
        z = jnp.zeros_like(w[:, 0])
        top = jnp.concatenate([w[:, 0], z], axis=2)
        bot = jnp.concatenate([z, w[:, 1]], axis=2)
        return jnp.concatenate([top, bot], axis=1)
    return jnp.concatenate([bd(wa), bd(wx)], axis=2)


def kernel(x, ffn1_norm, ffn1_w_gate, ffn1_w_up, ffn1_w_down, mix_norm, w_in, rwkv_mu, rwkv_w0, rwkv_w2, rwkv_a0, rwkv_a2, rwkv_g2, rwkv_k_k, rwkv_k_a, rwkv_r_k, rwkv_ln_w, rwkv_ln_b, lru_conv_w, lru_conv_b, lru_wa, lru_ba, lru_wx, lru_bx, lru_lam, lru_norm, w_out, ffn2_norm, ffn2_w_gate, ffn2_w_up, ffn2_w_down, final_norm):
    bsz, seq, d = x.shape
    depth = w_in.shape[0]
    wr = rwkv_w0.shape[1]
    wl = lru_lam.shape[1]
    n_w, n_a, n_g = rwkv_w2.shape[1], rwkv_a2.shape[1], rwkv_g2.shape[1]
    assert wr == wl and wr % SLAB == 0 and n_w + n_a == LANES and lru_wa.shape[2] == HEAD_DIM
    small_w = 4 * LANES
    g_pad = 2 * LANES
    assert n_g <= g_pad and (3 * wr) % small_w == 0 and seq % (2 * CHUNK) == 0

    row = lambda v: v.reshape(1, -1).astype(F32)
    h = x.reshape(bsz * seq, d)
    for l in range(depth):
        h = _ffn(h, row(ffn1_norm[l]), ffn1_w_gate[l].astype(BF16), ffn1_w_up[l].astype(BF16),
                 ffn1_w_down[l].astype(BF16), row(final_norm), final_norm=False)

        o_r, o_w = 0, wr
        o_k = o_w + n_w
        o_v = o_k + wr
        o_a = o_v + wr
        o_g = o_a + n_a
        o_l = o_g + n_g
        w = w_in[l]
        zpad = jnp.zeros((d, small_w - n_w - n_a - n_g), BF16)
        w_re = jnp.concatenate([w[:, o_r:o_r + wr].astype(BF16), w[:, o_k:o_k + wr].astype(BF16),
                                w[:, o_v:o_v + wr].astype(BF16), w[:, o_l:o_l + 2 * wl].astype(BF16),
                                w[:, o_w:o_w + n_w].astype(BF16), w[:, o_a:o_a + n_a].astype(BF16),
                                w[:, o_g:o_g + n_g].astype(BF16), zpad], axis=1)
        mu = rwkv_mu[l]
        mu_re = jnp.concatenate([mu[o_r:o_r + wr], mu[o_k:o_k + wr], mu[o_v:o_v + wr],
                                 jnp.zeros((2 * wl,), F32),
                                 mu[o_w:o_w + n_w], mu[o_a:o_a + n_a], mu[o_g:o_g + n_g],
                                 jnp.zeros((small_w - n_w - n_a - n_g,), F32)]).reshape(1, -1).astype(F32)
        vec = jnp.stack([rwkv_w0[l], rwkv_a0[l], rwkv_k_k[l], rwkv_k_a[l], rwkv_r_k[l].reshape(-1),
                         rwkv_ln_w[l], rwkv_ln_b[l], jnp.zeros((wr,), F32)]).astype(F32)
        w2p = _pad_rows(rwkv_w2[l], LANES, 0).astype(BF16)
        a2p = _pad_rows(rwkv_a2[l], LANES, n_w).astype(BF16)
        g2p = _pad_rows(rwkv_g2[l], g_pad, 0).astype(BF16)

        lvec = jnp.concatenate([lru_conv_w[l], jnp.stack([lru_conv_b[l], lru_ba[l], lru_bx[l],
                                                          lru_lam[l], lru_norm[l]]),
                                jnp.zeros((16 - CONV_WIDTH - 5, wl), F32)], axis=0).astype(F32)
        w_bd = _block_diag_pairs(lru_wa[l], lru_wx[l]).astype(BF16)

        p = _proj(h, row(mix_norm[l]), w_re, mu_re, rows_per_seq=seq)
        y_r, y_l = _mixers(p, vec, w2p, a2p, g2p, lvec, w_bd, width=wr, lru_x_col=3, lru_gate_col=4,
                           small_col=(3 * wr + 2 * wl) // small_w, small_w=small_w,
                           chunks_per_seq=seq // CHUNK)
        h = _outproj(h, y_r, y_l, w_out[l].astype(BF16))
        h = _ffn(h, row(ffn2_norm[l]), ffn2_w_gate[l].astype(BF16), ffn2_w_up[l].astype(BF16),
                 ffn2_w_down[l].astype(BF16), row(final_norm), final_norm=(l == depth - 1))
    return h.reshape(bsz, seq, d)
```

```python
import functools

import jax
import jax.numpy as jnp
from jax import lax
from jax.experimental import pallas as pl
from jax.experimental.pallas import tpu as pltpu

F32 = jnp.float32
BF16 = jnp.bfloat16

HEAD_DIM = 64
CONV_WIDTH = 4
LRU_C = 8.0
NORM_EPS = 1e-6
DECAY_SCALE = 0.6065306597126334
GN_EPS = 64e-5

LANES = 128
CHUNK = 64
PAIR = LANES // HEAD_DIM
SLAB = 256
VMEM_PHYSICAL_V7X = 64 * 1024 * 1024
VMEM_LIMIT = VMEM_PHYSICAL_V7X - 4 * 1024 * 1024
VMEM_LIMIT_FFN = VMEM_PHYSICAL_V7X - 1024 * 1024


def _dot(a, b):
    return jnp.dot(a, b, preferred_element_type=F32)


def _dot_nt(a, b):
    return lax.dot_general(a, b, (((1,), (1,)), ((), ())), preferred_element_type=F32)


def _dot_tn(a, b):
    return lax.dot_general(a, b, (((0,), (0,)), ((), ())), preferred_element_type=F32)


def _sigmoid(x):
    return 0.5 * (jnp.tanh(0.5 * x) + 1.0)


def _softplus(x):
    return jnp.maximum(x, 0.0) + jnp.log(1.0 + jnp.exp(-jnp.abs(x)))


def _rms_norm(x, g):
    ms = jnp.mean(x * x, axis=-1, keepdims=True)
    return x * lax.rsqrt(ms + NORM_EPS) * g


def _ffn_kernel(h_ref, g_ref, wg_ref, wu_ref, wd_ref, fg_ref, o_ref, xn_ref, *, final_norm):
    j = pl.program_id(1)

    @pl.when(j == 0)
    def _():
        xn_ref[...] = _rms_norm(h_ref[...], g_ref[...]).astype(BF16)
        o_ref[...] = jnp.zeros_like(o_ref)

    xn = xn_ref[...]
    gate = _dot(xn, wg_ref[...])
    up = _dot(xn, wu_ref[...])
    act = (gate * _sigmoid(gate) * up).astype(BF16)
    o_ref[...] += _dot(act, wd_ref[...])

    @pl.when(j == pl.num_programs(1) - 1)
    def _():
        out = h_ref[...] + 0.5 * o_ref[...]
        if final_norm:
            out = _rms_norm(out, fg_ref[...])
        o_ref[...] = out


def _ffn(h, g, wg, wu, wd, fg, *, final_norm, tm=1024, tf=512):
    t, d = h.shape
    dff = wg.shape[1]
    grid = (t // tm, dff // tf)
    return pl.pallas_call(
        functools.partial(_ffn_kernel, final_norm=final_norm),
        grid=grid,
        in_specs=[
            pl.BlockSpec((tm, d), lambda i, j: (i, 0)),
            pl.BlockSpec((1, d), lambda i, j: (0, 0)),
            pl.BlockSpec((d, tf), lambda i, j: (0, j)),
            pl.BlockSpec((d, tf), lambda i, j: (0, j)),
            pl.BlockSpec((tf, d), lambda i, j: (j, 0)),
            pl.BlockSpec((1, d), lambda i, j: (0, 0)),
        ],
        out_specs=pl.BlockSpec((tm, d), lambda i, j: (i, 0)),
        out_shape=jax.ShapeDtypeStruct((t, d), F32),
        scratch_shapes=[pltpu.VMEM((tm, d), BF16)],
        compiler_params=pltpu.CompilerParams(
            dimension_semantics=("parallel", "arbitrary"), vmem_limit_bytes=VMEM_LIMIT_FFN),
        name="ffn",
    )(h, g, wg, wu, wd, fg)


def _proj_kernel(h_ref, g_ref, w_ref, mu_ref, o_ref, xn_ref, last_ref, *, tiles_per_seq):
    i, j = pl.program_id(0), pl.program_id(1)

    @pl.when(j == 0)
    def _():
        xn_ref[...] = _rms_norm(h_ref[...], g_ref[...]).astype(BF16)

    @pl.when(i == 0)
    def _():
        last_ref[j] = jnp.zeros(last_ref.shape[1:], F32)

    z = _dot(xn_ref[...], w_ref[...])
    above = jnp.where(i % tiles_per_seq == 0, 0.0, last_ref[j, 0:1, :])
    last_ref[j, 0:1, :] = z[z.shape[0] - 1:, :]
    o_ref[...] = (z + (_shift_rows(z, above) - z) * mu_ref[...]).astype(o_ref.dtype)


def _proj(h, g, w, mu, *, rows_per_seq, tm=512, tn=2816):
    t, d = h.shape
    n = w.shape[1]
    assert rows_per_seq % tm == 0
    return pl.pallas_call(
        functools.partial(_proj_kernel, tiles_per_seq=rows_per_seq // tm),
        grid=(t // tm, n // tn),
        in_specs=[
            pl.BlockSpec((tm, d), lambda i, j: (i, 0)),
            pl.BlockSpec((1, d), lambda i, j: (0, 0)),
            pl.BlockSpec((d, tn), lambda i, j: (0, j)),
            pl.BlockSpec((1, tn), lambda i, j: (0, j)),
        ],
        out_specs=pl.BlockSpec((tm, tn), lambda i, j: (i, j)),
        out_shape=jax.ShapeDtypeStruct((t, n), BF16),
        scratch_shapes=[pltpu.VMEM((tm, d), BF16),
                        pltpu.VMEM((n // tn, 8, tn), F32)],
        compiler_params=pltpu.CompilerParams(
            dimension_semantics=("arbitrary", "arbitrary"), vmem_limit_bytes=VMEM_LIMIT),
        name="proj",
    )(h, g, w, mu)


def _outproj_kernel(h_ref, ya_ref, yb_ref, w_ref, o_ref):
    ka = ya_ref.shape[1]
    o_ref[...] = (h_ref[...] + _dot(ya_ref[...], w_ref[:ka, :]) + _dot(yb_ref[...], w_ref[ka:, :]))


def _outproj(h, ya, yb, w, *, tm=512):
    t, d = h.shape
    ka, kb = ya.shape[1], yb.shape[1]
    return pl.pallas_call(
        _outproj_kernel,
        grid=(t // tm,),
        in_specs=[
            pl.BlockSpec((tm, d), lambda i: (i, 0)),
            pl.BlockSpec((tm, ka), lambda i: (i, 0)),
            pl.BlockSpec((tm, kb), lambda i: (i, 0)),
            pl.BlockSpec((ka + kb, d), lambda i: (0, 0)),
        ],
        out_specs=pl.BlockSpec((tm, d), lambda i: (i, 0)),
        out_shape=jax.ShapeDtypeStruct((t, d), F32),
        compiler_params=pltpu.CompilerParams(
            dimension_semantics=("parallel",), vmem_limit_bytes=VMEM_LIMIT),
        name="outproj",
    )(h, ya, yb, w)


def _head_sum(x, ones_bd, exact):
    hi = x.astype(BF16)
    if not exact:
        return _dot(hi, ones_bd)
    r = x.shape[0]
    lo = (x - hi.astype(F32)).astype(BF16)
    z = _dot(jnp.concatenate([hi, lo], axis=0), ones_bd)
    return z[:r] + z[r:]


def _shift_rows(x, carry_row):
    rolled = pltpu.roll(x, 1, 0)
    row = lax.broadcasted_iota(jnp.int32, x.shape, 0)
    return jnp.where(row == 0, carry_row, rolled)


def _mixer_kernel(x0_ref, s0_ref, xn_ref, sn_ref, lx_ref, lg_ref,
                  vec_ref, w2_ref, a2_ref, g2_ref, lvec_ref, wbd_ref,
                  o_ref, ol_ref,
                  hs_ref, ab_ref, bt_ref, kt_ref, rb_ref, vv_ref, bh_ref, kh_ref,
                  pc_ref, bon_ref, gg_ref, y_ref, pw_ref, tb_ref, pi_ref, aak_ref, arb_ref, ark_ref,
                  ext_ref, hc_ref, yl_ref, ssq_ref,
                  *, chunks_per_seq, n_steps):
    j = pl.program_id(0)
    width = o_ref.shape[1]
    n_slabs = width // SLAB
    heads_per_slab = SLAB // HEAD_DIM

    vec = vec_ref[...]
    w0, a0, k_k, k_a, r_k, ln_w, ln_b = (vec[i:i + 1, :] for i in range(7))

    lane_s = lax.broadcasted_iota(jnp.int32, (SLAB, SLAB), 1)
    sub_s = lax.broadcasted_iota(jnp.int32, (SLAB, SLAB), 0)
    strict = lane_s < sub_s
    incl = lane_s <= sub_s
    eye = jnp.where(lane_s == sub_s, 1.0, 0.0).astype(BF16)
    ones_bd = jnp.where((lane_s // HEAD_DIM) == (sub_s // HEAD_DIM), 1.0, 0.0).astype(BF16)
    head_of_lane = lax.broadcasted_iota(jnp.int32, (CHUNK, SLAB), 1) // HEAD_DIM
    row_c = lax.broadcasted_iota(jnp.int32, (CHUNK, SLAB), 0)
    cols = [slice(q * SLAB, (q + 1) * SLAB) for q in range(n_slabs)]
    slabs = range(n_slabs)
    n_sq = CHUNK.bit_length() - 2
    halves = (slice(0, CHUNK), slice(CHUNK, 2 * CHUNK))

    def prep_ops(x_ref, s_ref, rows, slot):
        small = {}

        def lora_inputs():
            s1 = s_ref[rows, 0:LANES].astype(F32)
            s2 = s_ref[rows, LANES:LANES + g2_ref.shape[0]].astype(F32)
            small["tanh_w"] = jnp.tanh(s1).astype(BF16)
            small["lin_a"] = s_ref[rows, 0:LANES]
            small["sig_g"] = _sigmoid(s2).astype(BF16)

        val = {}

        def load(q):
            lo = q * SLAB
            val[q, "r"] = x_ref[rows, lo:lo + SLAB].astype(F32)
            val[q, "k"] = x_ref[rows, width + lo:width + lo + SLAB].astype(F32)
            vv_ref[slot, :, cols[q]] = x_ref[rows, 2 * width + lo:2 * width + lo + SLAB]
            val[q, "v"] = x_ref[rows, 2 * width + lo:2 * width + lo + SLAB].astype(F32)

        def lora(q):
            c = cols[q]
            z = w0[:, c] + _dot(small["tanh_w"], w2_ref[:, c])
            val[q, "a"] = _sigmoid(a0[:, c] + _dot(small["lin_a"], a2_ref[:, c]))
            gg_ref[slot, :, c] = _dot(small["sig_g"], g2_ref[:, c])
            val[q, "ld"] = (-DECAY_SCALE) * _sigmoid(z)

        def keys(q):
            c = cols[q]
            k = val[q, "k"]
            kk = k * k_k[:, c]
            val[q, "kk"] = kk * lax.rsqrt(jnp.maximum(_head_sum(kk * kk, ones_bd, exact=True), 1e-24))
            val[q, "k"] = k * (1.0 + (val[q, "a"] - 1.0) * k_a[:, c])

        def bonus(q):
            c = cols[q]
            rk = val[q, "r"] * val[q, "k"] * r_k[:, c]
            bon_ref[slot, :, c] = _head_sum(rk, ones_bd, exact=False) * val.pop((q, "v"))

        def decay(q):
            cum = val[q, "ld"]
            sh = 1
            while sh < CHUNK:
                cum = cum + jnp.where(row_c >= sh, pltpu.roll(cum, sh, 0), 0.0)
                sh *= 2
            val[q, "cum"] = cum

        def scale_in(q):
            c = cols[q]
            cum, ld, kk, r = val[q, "cum"], val.pop((q, "ld")), val[q, "kk"], val.pop((q, "r"))
            ab_ref[slot, :, c] = (-kk * jnp.exp(cum - ld)).astype(BF16)
            rb_ref[slot, :, c] = (r * jnp.exp(cum)).astype(BF16)

        def scale_out(q):
            c = cols[q]
            cum, kk, a, k = val.pop((q, "cum")), val.pop((q, "kk")), val.pop((q, "a")), val.pop((q, "k"))
            tot = cum[CHUNK - 1:CHUNK, :]
            kb = kk * a
            e_inv = jnp.exp(-cum)
            e_end = jnp.exp(tot - cum)
            bt_ref[slot, :, c] = (kb * e_inv).astype(BF16)
            kt_ref[slot, :, c] = (k * e_inv).astype(BF16)
            bh_ref[slot, :, c] = (kb * e_end).astype(BF16)
            kh_ref[slot, :, c] = (k * e_end).astype(BF16)
            pc_ref[slot, 0:1, c] = jnp.exp(tot)

        stages = (load, lora, keys, bonus, decay, scale_in, scale_out)
        return [lora_inputs] + [functools.partial(stage, q) for q in slabs for stage in stages]

    def stack(x):
        zero = jnp.zeros_like(x)
        return jnp.concatenate([jnp.where(head_of_lane == hd, x, zero) for hd in range(heads_per_slab)],
                               axis=0)

    def chain_ops(slot, half, ybuf):
        live = {}

        def gram(q):
            xx = jnp.concatenate([stack(ab_ref[slot, :, cols[q]]), stack(rb_ref[slot, :, cols[q]])], axis=0)
            gb = _dot_nt(xx, stack(bt_ref[slot, :, cols[q]]))
            gk = _dot_nt(xx, stack(kt_ref[slot, :, cols[q]]))
            a_ab = jnp.where(strict, gb[:SLAB], 0.0).astype(BF16)
            pw_ref[half, 0, q] = a_ab
            tb_ref[half, q] = a_ab + eye
            aak_ref[half, q] = jnp.where(strict, gk[:SLAB], 0.0).astype(BF16)
            arb_ref[half, q] = jnp.where(incl, gb[SLAB:], 0.0).astype(BF16)
            ark_ref[half, q] = jnp.where(incl, gk[SLAB:], 0.0).astype(BF16)

        def square(s, q):
            p2 = _dot(pw_ref[half, s % 2, q], pw_ref[half, s % 2, q]).astype(BF16)
            pw_ref[half, (s + 1) % 2, q] = p2
            pi_ref[half, q] = p2 + eye

        def extend(s, q):
            tb_ref[half, q] = _dot(tb_ref[half, q], pi_ref[half, q]).astype(BF16)

        def project(q):
            vs = stack(vv_ref[slot, :, cols[q]])
            xx = jnp.concatenate([stack(ab_ref[slot, :, cols[q]]), stack(rb_ref[slot, :, cols[q]])], axis=0)
            live[q] = (vs, _dot_nt(xx, hs_ref[q].astype(BF16)))

        def rhs(q):
            vs, xh = live[q]
            live[q] = (vs, xh[SLAB:], (xh[:SLAB] + _dot(aak_ref[half, q], vs)).astype(BF16))

        def solve(q):
            vs, xrh, b = live[q]
            live[q] = (vs, xrh, _dot(tb_ref[half, q], b).astype(BF16))

        def emit(q):
            vs, xrh, u = live[q]
            y_st = xrh + _dot(jnp.concatenate([arb_ref[half, q], ark_ref[half, q]], axis=1),
                              jnp.concatenate([u, vs], axis=0))
            y = y_st[0:CHUNK]
            for hd in range(1, heads_per_slab):
                y = y + y_st[hd * CHUNK:(hd + 1) * CHUNK]
            y_ref[ybuf, half, :, cols[q]] = y

        def advance(q):
            vs, _, u = live.pop(q)
            pc = pc_ref[slot, 0:1, cols[q]]
            hs_ref[q] = hs_ref[q] * pc + _dot_tn(
                jnp.concatenate([u, vs], axis=0),
                jnp.concatenate([stack(bh_ref[slot, :, cols[q]]), stack(kh_ref[slot, :, cols[q]])], axis=0))

        free = [gram]
        for s in range(n_sq):
            free += [functools.partial(square, s), functools.partial(extend, s)]
        bound = [project, rhs, solve, emit, advance]
        per_slab = lambda stages: [[functools.partial(stage, q) for q in slabs] for stage in stages]
        return per_slab(free), per_slab(bound)

    def post_ops(slot, half, ybuf):
        val = {}
        inv_n = 1.0 / HEAD_DIM

        def center(q):
            y = y_ref[ybuf, half, :, cols[q]]
            val[q] = y - _head_sum(y, ones_bd, exact=False) * inv_n

        def scale(q):
            c = cols[q]
            yc = val.pop(q)
            var = _head_sum(yc * yc, ones_bd, exact=False) * inv_n
            yn = yc * lax.rsqrt(var + GN_EPS) * ln_w[:, c] + ln_b[:, c]
            o_ref[halves[half], c] = ((yn + bon_ref[slot, :, c]) * gg_ref[slot, :, c]).astype(o_ref.dtype)

        return [functools.partial(stage, q) for q in slabs for stage in (center, scale)]

    def lru_ops():
        n_rows = 2 * CHUNK
        pad = 8
        lvec = lvec_ref[...]
        conv_w = [lvec[i:i + 1, :] for i in range(CONV_WIDTH)]
        conv_b, b_a, b_x, lam, norm_g = (lvec[i:i + 1, :] for i in range(CONV_WIDTH, CONV_WIDTH + 5))
        row8 = lax.broadcasted_iota(jnp.int32, (8, LANES), 0)
        lane_cols = [slice(s * LANES, (s + 1) * LANES) for s in range(width // LANES)]
        val = {}

        def gates(s):
            ln = lane_cols[s]
            x = lx_ref[:, ln].astype(F32)
            ext_ref[pad:pad + n_rows, ln] = x
            xc = conv_b[:, ln] + conv_w[CONV_WIDTH - 1][:, ln] * x
            for i in range(CONV_WIDTH - 1):
                xc = xc + conv_w[i][:, ln] * ext_ref[pl.ds(pad - (CONV_WIDTH - 1) + i, n_rows), ln]
            ext_ref[0:pad, ln] = ext_ref[n_rows:n_rows + pad, ln]
            z = _dot(xc.astype(BF16), wbd_ref[s])
            r = _sigmoid(z[:, :LANES] + b_a[:, ln])
            i_gate = _sigmoid(z[:, LANES:] + b_x[:, ln])
            log_a = (-LRU_C) * r * _softplus(-lam[:, ln])
            a = jnp.exp(log_a)
            val[s] = (a, jnp.sqrt(-jnp.tanh(log_a) * (a * a + 1.0)) * (i_gate * xc))

        n_groups = n_rows // 8
        scan_parts = 4

        def scan(s, part):
            ln = lane_cols[s]
            a, u = val[s]
            if part == 0:
                val[s, "hc"] = hc_ref[:, ln]
                val[s, "h"] = []
            hc = val[s, "hc"]
            for g in range(part * n_groups // scan_parts, (part + 1) * n_groups // scan_parts):
                aa, uu = a[8 * g:8 * g + 8], u[8 * g:8 * g + 8]
                for sh in (1, 2, 4):
                    ok = row8 >= sh
                    uu = jnp.where(ok, aa * pltpu.roll(uu, sh, 0) + uu, uu)
                    aa = jnp.where(ok, aa * pltpu.roll(aa, sh, 0), aa)
                hh = aa * hc + uu
                val[s, "h"].append(hh)
                hc = jnp.broadcast_to(hh[7:8, :], (8, LANES))
            val[s, "hc"] = hc
            if part == scan_parts - 1:
                hc_ref[:, ln] = val.pop((s, "hc"))
                del val[s]

        def gate(s):
            ln = lane_cols[s]
            gt = lg_ref[:, ln].astype(F32)
            gelu = 0.5 * gt * (1.0 + jnp.tanh(0.7978845608028654 * (gt + 0.044715 * gt * gt * gt)))
            y = jnp.concatenate(val.pop((s, "h")), axis=0) * gelu
            yl_ref[:, ln] = y
            ssq_ref[...] = y * y if s == 0 else ssq_ref[...] + y * y

        def finish():
            ms = jnp.sum(ssq_ref[...], axis=-1, keepdims=True) * (1.0 / width)
            scale = lax.rsqrt(ms + NORM_EPS)
            for ln in lane_cols:
                ol_ref[:, ln] = (yl_ref[:, ln] * scale * norm_g[:, ln]).astype(ol_ref.dtype)

        ops = []
        for s in range(len(lane_cols)):
            ops += [functools.partial(gates, s)]
            ops += [functools.partial(scan, s, part) for part in range(scan_parts)]
            ops += [functools.partial(gate, s)]
        return ops + [finish]

    def interleave(main, fill):
        n, m = len(main), len(fill)
        done = 0
        for i, op in enumerate(main):
            op()
            if (i + 1) % n_slabs:
                continue
            while done < m and (done + 1) * n <= (i + 1) * (m + 1):
                fill[done]()
                done += 1
        for op in fill[done:]:
            op()

    slot_sets = ((0, 1), (2, 3))

    def step(parity):
        chain_slots, prep_slots = slot_sets[parity], slot_sets[1 - parity]
        free0, bound0 = chain_ops(chain_slots[0], 0, parity)
        free1, bound1 = chain_ops(chain_slots[1], 1, parity)
        main = [op for pair in zip(free0, free1) for stage in pair for op in stage]
        tail = [op for stage in bound0 + bound1 for op in stage]
        lru = lru_ops()
        fill = []
        for half in range(2):
            fill += post_ops(prep_slots[half], half, 1 - parity)
            fill += prep_ops(xn_ref, sn_ref, halves[half], prep_slots[half])
            fill += lru[half * len(lru) // 2:(half + 1) * len(lru) // 2]
        cut = len(fill) // 2
        interleave(main, fill[:cut])
        interleave(tail, fill[cut:])

    @pl.when(j == 0)
    def _():
        y_ref[1] = jnp.zeros_like(y_ref[1])
        for slot in slot_sets[1]:
            bon_ref[slot] = jnp.zeros_like(bon_ref[slot])
            gg_ref[slot] = jnp.zeros_like(gg_ref[slot])
        for op in prep_ops(x0_ref, s0_ref, halves[0], 0) + prep_ops(x0_ref, s0_ref, halves[1], 1):
            op()

    @pl.when((2 * j) % chunks_per_seq == 0)
    def _():
        hs_ref[...] = jnp.zeros_like(hs_ref)
        hc_ref[...] = jnp.zeros_like(hc_ref)
        ext_ref[0:8, :] = jnp.zeros((8, width), F32)

    for parity in range(2):
        @pl.when((j % 2 == parity) & (j < n_steps))
        def _():
            step(parity)

    @pl.when(j == n_steps)
    def _():
        last = n_steps % 2
        for half in range(2):
            for op in post_ops(slot_sets[1 - last][half], half, 1 - last):
                op()


def _mixers(p, vec, w2p, a2p, g2p, lvec, w_bd, *, width, lru_x_col, lru_gate_col, small_col, small_w,
            chunks_per_seq):
    t = p.shape[0]
    n_chunks = t // CHUNK
    assert chunks_per_seq % 2 == 0 and n_chunks % chunks_per_seq == 0
    n_slabs = width // SLAB
    n_steps = n_chunks // 2
    block = 2 * CHUNK
    views = []
    for idx in (lambda j: 0, lambda j: jnp.minimum(j + 1, n_steps - 1)):
        views.append(pl.BlockSpec((block, 3 * width), lambda j, idx=idx: (idx(j), 0)))
        views.append(pl.BlockSpec((block, small_w), lambda j, idx=idx: (idx(j), small_col)))
    current = lambda j: jnp.minimum(j, n_steps - 1)
    views.append(pl.BlockSpec((block, width), lambda j: (current(j), lru_x_col)))
    views.append(pl.BlockSpec((block, width), lambda j: (current(j), lru_gate_col)))
    full = lambda a: pl.BlockSpec(a.shape, lambda j: (0,) * a.ndim)
    n_slots = 4
    slot_bf16 = pltpu.VMEM((n_slots, CHUNK, width), BF16)
    slot_f32 = pltpu.VMEM((n_slots, CHUNK, width), F32)
    return pl.pallas_call(
        functools.partial(_mixer_kernel, chunks_per_seq=chunks_per_seq, n_steps=n_steps),
        grid=(n_steps + 1,),
        in_specs=views + [full(vec), full(w2p), full(a2p), full(g2p), full(lvec), full(w_bd)],
        out_specs=[pl.BlockSpec((block, width), lambda j: (jnp.maximum(j - 1, 0), 0)),
                   pl.BlockSpec((block, width), lambda j: (current(j), 0))],
        out_shape=[jax.ShapeDtypeStruct((t, width), BF16), jax.ShapeDtypeStruct((t, width), BF16)],
        scratch_shapes=[
            pltpu.VMEM((n_slabs, SLAB, SLAB), F32),
            slot_bf16, slot_bf16, slot_bf16, slot_bf16, slot_bf16, slot_bf16, slot_bf16,
            pltpu.VMEM((n_slots, 8, width), F32),
            slot_f32, slot_f32,
            pltpu.VMEM((2, 2, CHUNK, width), F32),
            pltpu.VMEM((2, 2, n_slabs, SLAB, SLAB), BF16),
            pltpu.VMEM((2, n_slabs, SLAB, SLAB), BF16),
            pltpu.VMEM((2, n_slabs, SLAB, SLAB), BF16),
            pltpu.VMEM((2, n_slabs, SLAB, SLAB), BF16), pltpu.VMEM((2, n_slabs, SLAB, SLAB), BF16),
            pltpu.VMEM((2, n_slabs, SLAB, SLAB), BF16),
            pltpu.VMEM((block + 8, width), F32),
            pltpu.VMEM((8, width), F32),
            pltpu.VMEM((block, width), F32),
            pltpu.VMEM((block, LANES), F32),
        ],
        compiler_params=pltpu.CompilerParams(
            dimension_semantics=("arbitrary",), vmem_limit_bytes=VMEM_LIMIT),
        name="mixers",
    )(p, p, p, p, p, p, vec, w2p, a2p, g2p, lvec, w_bd)


def _pad_rows(w, rows, at):
    out = jnp.zeros((rows, w.shape[1]), w.dtype)
    return out.at[at:at + w.shape[0]].set(w)


def _block_diag_pairs(wa, wx):
    def bd(w):
        h, n, _ = w.shape
        w = w.reshape(h // PAIR, PAIR, n, n)
        z = jnp.zeros_like(w[:, 0])
        top = jnp.concatenate([w[:, 0], z], axis=2)
        bot = jnp.concatenate([z, w[:, 1]], axis=2)
        return jnp.concatenate([top, bot], axis=1)
    return jnp.concatenate([bd(wa), bd(wx)], axis=2)


def kernel(x, ffn1_norm, ffn1_w_gate, ffn1_w_up, ffn1_w_down, mix_norm, w_in, rwkv_mu, rwkv_w0, rwkv_w2, rwkv_a0, rwkv_a2, rwkv_g2, rwkv_k_k, rwkv_k_a, rwkv_r_k, rwkv_ln_w, rwkv_ln_b, lru_conv_w, lru_conv_b, lru_wa, lru_ba, lru_wx, lru_bx, lru_lam, lru_norm, w_out, ffn2_norm, ffn2_w_gate, ffn2_w_up, ffn2_w_down, final_norm):
    bsz, seq, d = x.shape
    depth = w_in.shape[0]
    wr = rwkv_w0.shape[1]
    wl = lru_lam.shape[1]
    n_w, n_a, n_g = rwkv_w2.shape[1], rwkv_a2.shape[1], rwkv_g2.shape[1]
    assert wr == wl and wr % SLAB == 0 and n_w + n_a == LANES and lru_wa.shape[2] == HEAD_DIM
    small_w = 4 * LANES
    g_pad = 2 * LANES
    assert n_g <= g_pad and (3 * wr) % small_w == 0 and seq % (2 * CHUNK) == 0

    row = lambda v: v.reshape(1, -1).astype(F32)
    h = x.reshape(bsz * seq, d)
    for l in range(depth):
        h = _ffn(h, row(ffn1_norm[l]), ffn1_w_gate[l].astype(BF16), ffn1_w_up[l].astype(BF16),
                 ffn1_w_down[l].astype(BF16), row(final_norm), final_norm=False)

        o_r, o_w = 0, wr
        o_k = o_w + n_w
        o_v = o_k + wr
        o_a = o_v + wr
        o_g = o_a + n_a
        o_l = o_g + n_g
        w = w_in[l]
        zpad = jnp.zeros((d, small_w - n_w - n_a - n_g), BF16)
        w_re = jnp.concatenate([w[:, o_r:o_r + wr].astype(BF16), w[:, o_k:o_k + wr].astype(BF16),
                                w[:, o_v:o_v + wr].astype(BF16), w[:, o_l:o_l + 2 * wl].astype(BF16),
                                w[:, o_w:o_w + n_w].astype(BF16), w[:, o_a:o_a + n_a].astype(BF16),
                                w[:, o_g:o_g + n_g].astype(BF16), zpad], axis=1)
        mu = rwkv_mu[l]
        mu_re = jnp.concatenate([mu[o_r:o_r + wr], mu[o_k:o_k + wr], mu[o_v:o_v + wr],
                                 jnp.zeros((2 * wl,), F32),
                                 mu[o_w:o_w + n_w], mu[o_a:o_a + n_a], mu[o_g:o_g + n_g],
                                 jnp.zeros((small_w - n_w - n_a - n_g,), F32)]).reshape(1, -1).astype(F32)
        vec = jnp.stack([rwkv_w0[l], rwkv_a0[l], rwkv_k_k[l], rwkv_k_a[l], rwkv_r_k[l].reshape(-1),
                         rwkv_ln_w[l], rwkv_ln_b[l], jnp.zeros((wr,), F32)]).astype(F32)
        w2p = _pad_rows(rwkv_w2[l], LANES, 0).astype(BF16)
        a2p = _pad_rows(rwkv_a2[l], LANES, n_w).astype(BF16)
        g2p = _pad_rows(rwkv_g2[l], g_pad, 0).astype(BF16)

        lvec = jnp.concatenate([lru_conv_w[l], jnp.stack([lru_conv_b[l], lru_ba[l], lru_bx[l],
                                                          lru_lam[l], lru_norm[l]]),
                                jnp.zeros((16 - CONV_WIDTH - 5, wl), F32)], axis=0).astype(F32)
        w_bd = _block_diag_pairs(lru_wa[l], lru_wx[l]).astype(BF16)

        p = _proj(h, row(mix_norm[l]), w_re, mu_re, rows_per_seq=seq)
        y_r, y_l = _mixers(p, vec, w2p, a2p, g2p, lvec, w_bd, width=wr, lru_x_col=3, lru_gate_col=4,
                           small_col=(3 * wr + 2 * wl) // small_w, small_w=small_w,
                           chunks_per_seq=seq // CHUNK)
        h = _outproj(h, y_r, y_l, w_out[l].astype(BF16))
        h = _ffn(h, row(ffn2_norm[l]), ffn2_w_gate[l].astype(BF16), ffn2_w_up[l].astype(BF16),
                 ffn2_w_down[l].astype(BF16), row(final_norm), final_norm=(l == depth - 1))
    return h.reshape(bsz, seq, d)
```
